```python
import jax
import jax.numpy as jnp
from jax import lax
import numpy as np

D_MODEL = 1024
BATCH = 32
SEQ = 2048
DEPTH = 1

RET_HEADS = 4
RET_DK = 64
RET_DV = 128
RET_CHUNK = 128
RET_THETA = 10000.0
MOBA_HEADS = 8
MOBA_DH = 64
MOBA_BLOCK = 256
MOBA_TOPK = 3
MOBA_Q_CHUNK = 16
ROPE_THETA = 500000.0
ROPE_DIM = MOBA_DH // 4
RET_QK_W = RET_HEADS * RET_DK
RET_V_W = RET_HEADS * RET_DV
MOBA_W = MOBA_HEADS * MOBA_DH
D_MIX = RET_V_W + MOBA_W
D_IN = 2 * RET_QK_W + 2 * RET_V_W + 3 * MOBA_W
SPLITS = [int(v) for v in np.cumsum([RET_QK_W, RET_QK_W, RET_V_W, RET_V_W, MOBA_W, MOBA_W])]
N_EXPERTS = 32
TOP_K = 4
D_EXPERT = D_MODEL
SWIGLU_LIMIT = 7.0
SWIGLU_ALPHA = 1.702
MOE_BLOCK = 256
RMS_EPS = 1e-5
GN_EPS = 1e-6
NEG_INF = -1e30

kernel_name = "hybrid_retention_moba_moe_layer"


def rms_norm(x, w):
    xf = x.astype(jnp.float32)
    y = xf * lax.rsqrt(jnp.mean(xf * xf, axis=-1, keepdims=True) + RMS_EPS)
    return (y * w.astype(jnp.float32)).astype(x.dtype)


def rotary(x, pos, rot_dim, theta):
    half = rot_dim // 2
    inv_freq = theta ** (-2.0 * jnp.arange(half, dtype=jnp.float32) / rot_dim)
    ang = pos.astype(jnp.float32)[:, None] * inv_freq[None, :]
    cos = jnp.cos(ang)[None, :, None, :]
    sin = jnp.sin(ang)[None, :, None, :]
    xf = x.astype(jnp.float32)
    x1 = xf[..., :half]
    x2 = xf[..., half:rot_dim]
    out = jnp.concatenate([x1 * cos - x2 * sin, x1 * sin + x2 * cos, xf[..., rot_dim:]], axis=-1)
    return out.astype(x.dtype)


def retention(q, k, v):
    b, s, h, dk = q.shape
    dv = v.shape[-1]
    c = RET_CHUNK
    nc = s // c
    to_chunks = lambda t: t.reshape(b, nc, c, h, t.shape[-1]).transpose(0, 3, 1, 2, 4).astype(jnp.float32)
    qc = to_chunks(q)
    kc = to_chunks(k) * (dk ** -0.5)
    vc = to_chunks(v)
    log_g = jnp.log1p(-jnp.exp2(-5.0 - jnp.arange(h, dtype=jnp.float32)))
    idx = jnp.arange(c, dtype=jnp.float32)
    rel = idx[:, None] - idx[None, :]
    decay = jnp.where(rel >= 0, jnp.exp(log_g[:, None, None] * jnp.maximum(rel, 0.0)), 0.0)
    scores = jnp.einsum("bhcnd,bhcmd->bhcnm", qc, kc) * decay[:, None]
    inner = jnp.einsum("bhcnm,bhcme->bhcne", scores, vc)
    zeta = jnp.exp(log_g[:, None] * (c - 1.0 - idx))
    chunk_state = jnp.einsum("bhcmd,bhcme->bhcde", kc * zeta[:, None, :, None], vc)
    chunk_decay = jnp.exp(log_g * c)[:, None, None]

    def step(r, u):
        return chunk_decay * r + u, r

    _, r_prev = lax.scan(step, jnp.zeros((b, h, dk, dv), jnp.float32), chunk_state.transpose(2, 0, 1, 3, 4))
    r_prev = r_prev.transpose(1, 2, 0, 3, 4)
    xi = jnp.exp(log_g[:, None] * (idx + 1.0))
    cross = jnp.einsum("bhcnd,bhcde->bhcne", qc * xi[:, None, :, None], r_prev)
    return (inner + cross).transpose(0, 2, 3, 1, 4).reshape(b, s, h, dv)


def moba_attention(q, k, v):
    b, h, s, d = q.shape
    nb = -(-s // MOBA_BLOCK)
    sp = nb * MOBA_BLOCK
    pad = [(0, 0), (0, 0), (0, sp - s), (0, 0)]
    q = jnp.pad(q, pad)
    k = jnp.pad(k, pad)
    v = jnp.pad(v, pad)
    kb = k.reshape(b, h, nb, MOBA_BLOCK, d)
    vb = v.reshape(b, h, nb, MOBA_BLOCK, d)
    k_mean = jnp.mean(kb.astype(jnp.float32), axis=3)
    gate = jnp.einsum("bhsd,bhnd->bhsn", q.astype(jnp.float32), k_mean)
    q_blk = jnp.arange(sp) // MOBA_BLOCK
    fully_past = jnp.arange(nb)[None, :] < q_blk[:, None]
    gate = jnp.where(fully_past, gate, NEG_INF)
    k_sel = max(1, min(MOBA_TOPK, nb - 1))
    _, sel = lax.top_k(gate, k_sel)
    nq = sp // MOBA_Q_CHUNK
    qs = q.reshape(b, h, nq, MOBA_Q_CHUNK, d).transpose(2, 0, 1, 3, 4)
    sels = sel.reshape(b, h, nq, MOBA_Q_CHUNK, k_sel).transpose(2, 0, 1, 3, 4)
    gather = jax.vmap(jax.vmap(lambda blocks, ids: blocks[ids]))
    scale = d ** -0.5
    key_in_blk = jnp.arange(MOBA_BLOCK)
    slot_ids = jnp.arange(k_sel)

    def query_chunk(args):
        ci, q_c, sel_c = args
        q0 = ci * MOBA_Q_CHUNK
        blk = q0 // MOBA_BLOCK
        k_own = lax.dynamic_index_in_dim(kb, blk, axis=2, keepdims=False)
        v_own = lax.dynamic_index_in_dim(vb, blk, axis=2, keepdims=False)
        k_g = gather(kb, sel_c)
        v_g = gather(vb, sel_c)
        s_own = jnp.einsum("bhqd,bhld->bhql", q_c, k_own, preferred_element_type=jnp.float32) * scale
        q_in_blk = q0 % MOBA_BLOCK + jnp.arange(MOBA_Q_CHUNK)
        s_own = jnp.where(key_in_blk[None, :] <= q_in_blk[:, None], s_own, NEG_INF)
        s_sel = jnp.einsum("bhqd,bhqkld->bhqkl", q_c, k_g, preferred_element_type=jnp.float32) * scale
        s_sel = jnp.where((slot_ids < blk)[:, None], s_sel, NEG_INF)
        n_sel = k_sel * MOBA_BLOCK
        logits = jnp.concatenate([s_sel.reshape(b, h, MOBA_Q_CHUNK, n_sel), s_own], axis=-1)
        p = jax.nn.softmax(logits, axis=-1)
        p_sel = p[..., :n_sel].reshape(b, h, MOBA_Q_CHUNK, k_sel, MOBA_BLOCK).astype(v.dtype)
        p_own = p[..., n_sel:].astype(v.dtype)
        return (jnp.einsum("bhqkl,bhqkld->bhqd", p_sel, v_g, preferred_element_type=jnp.float32)
                + jnp.einsum("bhql,bhld->bhqd", p_own, v_own, preferred_element_type=jnp.float32))

    out = lax.map(query_chunk, (jnp.arange(nq), qs, sels))
    return out.transpose(1, 2, 0, 3, 4).reshape(b, h, sp, d)[:, :, :s]


def clamped_swiglu(u):
    gate, lin = jnp.split(u, 2, axis=-1)
    gate = jnp.minimum(gate, SWIGLU_LIMIT)
    lin = jnp.clip(lin, -SWIGLU_LIMIT, SWIGLU_LIMIT)
    return gate * jax.nn.sigmoid(SWIGLU_ALPHA * gate) * (lin + 1.0)


def moe_ffn(h, w_router, b_router, w1, b1, w2, b2):
    n, d = h.shape
    logits = jnp.dot(h, w_router, preferred_element_type=jnp.float32) + b_router.astype(jnp.float32)
    top_logit, top_e = lax.top_k(logits, TOP_K)
    gate = jax.nn.softmax(top_logit, axis=-1)
    n_pairs = n * TOP_K
    e_flat = top_e.reshape(n_pairs)
    order = jnp.argsort(e_flat).astype(jnp.int32)
    e_sorted = e_flat[order]
    counts = jnp.bincount(e_flat, length=N_EXPERTS).astype(jnp.int32)
    padded = (counts + MOE_BLOCK - 1) // MOE_BLOCK * MOE_BLOCK
    start = jnp.cumsum(counts) - counts
    pad_end = jnp.cumsum(padded)
    pad_start = pad_end - padded
    dest = pad_start[e_sorted] + jnp.arange(n_pairs, dtype=jnp.int32) - start[e_sorted]
    n_blocks = -(-n_pairs // MOE_BLOCK) + N_EXPERTS
    slot_pair = jnp.full((n_blocks * MOE_BLOCK,), n_pairs, dtype=jnp.int32).at[dest].set(order)
    slot_tok = slot_pair // TOP_K
    slot_gate = jnp.concatenate([gate.reshape(n_pairs), jnp.zeros((1,), gate.dtype)])[slot_pair]
    block_expert = jnp.minimum(
        jnp.searchsorted(pad_end, jnp.arange(n_blocks, dtype=jnp.int32) * MOE_BLOCK, side="right"),
        N_EXPERTS - 1)
    h_pad = jnp.concatenate([h, jnp.zeros((1, d), h.dtype)], axis=0)

    def expert_block(acc, xs):
        tok, g, e = xs
        u = jnp.dot(h_pad[tok], w1[e], preferred_element_type=jnp.float32) + b1[e]
        y = jnp.dot(clamped_swiglu(u).astype(h.dtype), w2[e], preferred_element_type=jnp.float32) + b2[e]
        return acc.at[tok].add(y * g[:, None]), None

    acc, _ = lax.scan(expert_block, jnp.zeros((n + 1, d), jnp.float32),
                      (slot_tok.reshape(n_blocks, MOE_BLOCK), slot_gate.reshape(n_blocks, MOE_BLOCK), block_expert))
    return acc[:n].astype(h.dtype)


def setup_inputs(seed: int = 0) -> dict:
    key = jax.random.key(seed)
    ks = jax.random.split(key, 13)
    f32 = jnp.float32
    nrm = lambda k, shape, scale: jax.random.normal(k, shape, f32) * scale
    return {
        "x": nrm(ks[0], (BATCH, SEQ, D_MODEL), 1.0),
        "norm1_w": 1.0 + nrm(ks[1], (DEPTH, D_MODEL), 0.02),
        "w_in": nrm(ks[2], (DEPTH, D_MODEL, D_IN), D_MODEL ** -0.5),
        "ret_norm_w": 1.0 + nrm(ks[3], (DEPTH, RET_V_W), 0.02),
        "w_out": nrm(ks[4], (DEPTH, D_MIX, D_MODEL), D_MIX ** -0.5),
        "norm2_w": 1.0 + nrm(ks[5], (DEPTH, D_MODEL), 0.02),
        "w_router": nrm(ks[6], (DEPTH, D_MODEL, N_EXPERTS), D_MODEL ** -0.5),
        "b_router": nrm(ks[7], (DEPTH, N_EXPERTS), 0.01),
        "w1": nrm(ks[8], (DEPTH, N_EXPERTS, D_MODEL, 2 * D_EXPERT), D_MODEL ** -0.5),
        "b1": nrm(ks[9], (DEPTH, N_EXPERTS, 2 * D_EXPERT), 0.01),
        "w2": nrm(ks[10], (DEPTH, N_EXPERTS, D_EXPERT, D_MODEL), D_EXPERT ** -0.5),
        "b2": nrm(ks[11], (DEPTH, N_EXPERTS, D_MODEL), 0.01),
        "final_norm_w": 1.0 + nrm(ks[12], (D_MODEL,), 0.02),
    }


def reference(x, norm1_w, w_in, ret_norm_w, w_out, norm2_w, w_router, b_router, w1, b1, w2, b2, final_norm_w):
    b, s, d = x.shape
    pos = jnp.arange(s)
    for i in range(DEPTH):
        h = rms_norm(x, norm1_w[i])
        proj = jnp.dot(h, w_in[i])
        rq, rk, rv, rg, mq, mk, mv = jnp.split(proj, SPLITS, axis=-1)
        rq = rotary(rq.reshape(b, s, RET_HEADS, RET_DK), pos, RET_DK, RET_THETA)
        rk = rotary(rk.reshape(b, s, RET_HEADS, RET_DK), pos, RET_DK, RET_THETA)
        ret = retention(rq, rk, rv.reshape(b, s, RET_HEADS, RET_DV))
        mu = jnp.mean(ret, axis=-1, keepdims=True)
        var = jnp.mean(jnp.square(ret - mu), axis=-1, keepdims=True)
        ret = ((ret - mu) * lax.rsqrt(var + GN_EPS)).reshape(b, s, RET_V_W) * ret_norm_w[i].astype(jnp.float32)
        ret = (jax.nn.silu(rg.astype(jnp.float32)) * ret).astype(x.dtype)
        mq = rotary(mq.reshape(b, s, MOBA_HEADS, MOBA_DH), pos, ROPE_DIM, ROPE_THETA).transpose(0, 2, 1, 3)
        mk = rotary(mk.reshape(b, s, MOBA_HEADS, MOBA_DH), pos, ROPE_DIM, ROPE_THETA).transpose(0, 2, 1, 3)
        mv = mv.reshape(b, s, MOBA_HEADS, MOBA_DH).transpose(0, 2, 1, 3)
        moba = moba_attention(mq, mk, mv).transpose(0, 2, 1, 3).reshape(b, s, MOBA_W).astype(x.dtype)
        x = x + jnp.dot(jnp.concatenate([ret, moba], axis=-1), w_out[i])
        h2 = rms_norm(x, norm2_w[i]).reshape(b * s, d)
        x = x + moe_ffn(h2, w_router[i], b_router[i], w1[i], b1[i], w2[i], b2[i]).reshape(b, s, d)
    return rms_norm(x, final_norm_w)
```

```python
import functools

import numpy as np
import jax
import jax.numpy as jnp
from jax import lax
from jax.experimental import pallas as pl
from jax.experimental.pallas import tpu as pltpu

D_MODEL = 1024
RET_HEADS = 4
RET_DK = 64
RET_DV = 128
RET_CHUNK = 128
RET_THETA = 10000.0
MOBA_HEADS = 8
MOBA_DH = 64
MOBA_BLOCK = 256
MOBA_TOPK = 3
ROPE_THETA = 500000.0
ROPE_DIM = MOBA_DH // 4
RET_QK_W = RET_HEADS * RET_DK
RET_V_W = RET_HEADS * RET_DV
MOBA_W = MOBA_HEADS * MOBA_DH
N_EXPERTS = 32
TOP_K = 4
D_EXPERT = D_MODEL
SWIGLU_LIMIT = 7.0
SWIGLU_ALPHA = 1.702
RMS_EPS = 1e-5
GN_EPS = 1e-6
NEG_INF = -1e30

SUBLANES = 8
LANES = 128
ROW_TILES = D_MODEL // LANES
VMEM_LIMIT = 48 * 1024 * 1024

F32 = jnp.float32
BF16 = jnp.bfloat16
NT_DIMS = (((1,), (1,)), ((), ()))
TN_DIMS = (((0,), (0,)), ((), ()))


def _params(sem, vmem=VMEM_LIMIT):
    return pltpu.CompilerParams(dimension_semantics=sem, vmem_limit_bytes=vmem)


def _rotate(x, cos, sin_lo, sin_hi, half):
    width = x.shape[-1]
    nxt = pltpu.roll(x, width - half, axis=1)
    prv = pltpu.roll(x, half, axis=1)
    return x * cos + nxt * sin_lo + prv * sin_hi


def _inproj_kernel(x_ref, nw_ref, w_ref, wvt_ref, rc_ref, rlo_ref, rhi_ref, mc_ref, mlo_ref, mhi_ref,
                   rq_ref, rk_ref, rv_ref, rg_ref, mq_ref, mk_ref, mvt_ref):
    x = x_ref[...]
    ms = jnp.mean(x * x, axis=-1, keepdims=True)
    h = (x * lax.rsqrt(ms + RMS_EPS) * nw_ref[...]).astype(BF16)
    proj = jnp.dot(h, w_ref[...], preferred_element_type=F32)
    o = 0
    rq = proj[:, o:o + RET_QK_W]; o += RET_QK_W
    rk = proj[:, o:o + RET_QK_W]; o += RET_QK_W
    rv = proj[:, o:o + RET_V_W]; o += RET_V_W
    rg = proj[:, o:o + RET_V_W]; o += RET_V_W
    mq = proj[:, o:o + MOBA_W]; o += MOBA_W
    mk = proj[:, o:o + MOBA_W]
    rc, rlo, rhi = rc_ref[...], rlo_ref[...], rhi_ref[...]
    mc, mlo, mhi = mc_ref[...], mlo_ref[...], mhi_ref[...]
    rq_ref[...] = _rotate(rq, rc, rlo, rhi, RET_DK // 2).astype(BF16)
    rk_ref[...] = _rotate(rk, rc, rlo, rhi, RET_DK // 2).astype(BF16)
    rv_ref[...] = rv.astype(BF16)
    rg_ref[...] = rg.astype(BF16)
    mq_ref[...] = _rotate(mq, mc, mlo, mhi, ROPE_DIM // 2).astype(BF16)
    mk_ref[...] = _rotate(mk, mc, mlo, mhi, ROPE_DIM // 2).astype(BF16)
    mvt_ref[0] = lax.dot_general(wvt_ref[...], h, NT_DIMS, preferred_element_type=F32).astype(BF16)


def _rotary_tables(seq, rot_dim, theta, head_dim, heads):
    half = rot_dim // 2
    inv_freq = theta ** (-2.0 * jnp.arange(half, dtype=F32) / rot_dim)
    ang = jnp.arange(seq).astype(F32)[:, None] * inv_freq[None, :]
    cos, sin = jnp.cos(ang), jnp.sin(ang)
    rest = head_dim - rot_dim
    ones = jnp.ones((seq, rest), F32)
    zeros = jnp.zeros((seq, rest), F32)
    zh = jnp.zeros((seq, half), F32)
    cos_t = jnp.concatenate([cos, cos, ones], axis=-1)
    lo_t = jnp.concatenate([-sin, zh, zeros], axis=-1)
    hi_t = jnp.concatenate([zh, sin, zeros], axis=-1)
    tile = lambda t: jnp.tile(t, (1, heads))
    return tile(cos_t), tile(lo_t), tile(hi_t)


def _in_projection(x2, norm_w, w_in, batch, seq, tm):
    n = x2.shape[0]
    st = seq // tm
    w_main = w_in[:, :D_MODEL * 3 - MOBA_W].astype(BF16)
    w_vt = w_in[:, D_MODEL * 3 - MOBA_W:].T.astype(BF16)
    wm = w_main.shape[1]
    r_tabs = _rotary_tables(seq, RET_DK, RET_THETA, RET_DK, RET_HEADS)
    m_tabs = _rotary_tables(seq, ROPE_DIM, ROPE_THETA, MOBA_DH, MOBA_HEADS)
    row = lambda w: pl.BlockSpec((tm, w), lambda b, s: (b * st + s, 0))
    pos = lambda w: pl.BlockSpec((tm, w), lambda b, s: (s, 0))
    full = lambda a: pl.BlockSpec(a.shape, lambda b, s: (0,) * a.ndim)
    nw = norm_w.reshape(1, D_MODEL)
    out_shape = (
        jax.ShapeDtypeStruct((n, RET_QK_W), BF16), jax.ShapeDtypeStruct((n, RET_QK_W), BF16),
        jax.ShapeDtypeStruct((n, RET_V_W), BF16), jax.ShapeDtypeStruct((n, RET_V_W), BF16),
        jax.ShapeDtypeStruct((n, MOBA_W), BF16), jax.ShapeDtypeStruct((n, MOBA_W), BF16),
        jax.ShapeDtypeStruct((batch, MOBA_W, seq), BF16),
    )
    return pl.pallas_call(
        _inproj_kernel,
        grid=(batch, st),
        in_specs=[row(D_MODEL), full(nw), full(w_main), full(w_vt),
                  pos(RET_QK_W), pos(RET_QK_W), pos(RET_QK_W), pos(MOBA_W), pos(MOBA_W), pos(MOBA_W)],
        out_specs=(row(RET_QK_W), row(RET_QK_W), row(RET_V_W), row(RET_V_W), row(MOBA_W), row(MOBA_W),
                   pl.BlockSpec((1, MOBA_W, tm), lambda b, s: (b, 0, s))),
        out_shape=out_shape,
        compiler_params=_params(("parallel", "parallel")),
        name="in_projection",
    )(x2, nw, w_main, w_vt, *r_tabs, *m_tabs)


def _retention_kernel(q_ref, k_ref, v_ref, g_ref, decay_ref, xi_ref, zeta_ref, gnw_ref, o_ref, state_ref,
                      *, n_chunks, chunk_decay):
    c = RET_CHUNK
    state_ref[...] = jnp.zeros_like(state_ref)

    def chunk(ci, carry):
        r0 = pl.multiple_of(ci * c, c)
        q = q_ref[pl.ds(r0, c), :]
        k = k_ref[pl.ds(r0, c), :]
        v = v_ref[pl.ds(r0, c), :]
        g = g_ref[pl.ds(r0, c), :].astype(F32)
        qx = (q.astype(F32) * xi_ref[...]).astype(BF16)
        kz = (k.astype(F32) * zeta_ref[...]).astype(BF16)
        outs = []
        for h in range(RET_HEADS):
            qs = slice(h * RET_DK, (h + 1) * RET_DK)
            vs = slice(h * RET_DV, (h + 1) * RET_DV)
            vh = v[:, vs]
            scores = lax.dot_general(q[:, qs], k[:, qs], NT_DIMS, preferred_element_type=F32) * decay_ref[h]
            inner = jnp.dot(scores.astype(BF16), vh, preferred_element_type=F32)
            state = state_ref[h]
            cross = jnp.dot(qx[:, qs], state.astype(BF16), preferred_element_type=F32)
            state_ref[h] = chunk_decay[h] * state + lax.dot_general(
                kz[:, qs], vh, TN_DIMS, preferred_element_type=F32)
            ret = inner + cross
            mu = jnp.mean(ret, axis=-1, keepdims=True)
            cen = ret - mu
            var = jnp.mean(cen * cen, axis=-1, keepdims=True)
            outs.append(cen * lax.rsqrt(var + GN_EPS))
        normed = jnp.concatenate(outs, axis=-1) * gnw_ref[...]
        o_ref[pl.ds(r0, c), :] = (g * jax.nn.sigmoid(g) * normed).astype(BF16)
        return carry

    lax.fori_loop(0, n_chunks, chunk, 0)


def _retention_consts():
    h = np.arange(RET_HEADS, dtype=np.float32)
    log_g = np.log1p(-np.exp2(-5.0 - h)).astype(np.float32)
    idx = np.arange(RET_CHUNK, dtype=np.float32)
    rel = idx[:, None] - idx[None, :]
    scale = np.float32(RET_DK ** -0.5)
    decay = np.where(rel >= 0, np.exp(log_g[:, None, None] * np.maximum(rel, 0.0)), 0.0).astype(np.float32) * scale
    zeta = np.exp(log_g[:, None] * (RET_CHUNK - 1.0 - idx)).astype(np.float32) * scale
    xi = np.exp(log_g[:, None] * (idx + 1.0)).astype(np.float32)
    spread = lambda t: np.repeat(t.T, RET_DK, axis=1)
    chunk_decay = tuple(float(v) for v in np.exp(log_g * RET_CHUNK).astype(np.float32))
    return decay, spread(xi), spread(zeta), chunk_decay


def _retention(rq, rk, rv, rg, gn_w, batch, seq):
    n = rq.shape[0]
    decay, xi, zeta, chunk_decay = _retention_consts()
    gnw = gn_w.reshape(1, RET_V_W).astype(F32)
    seq_blk = lambda w: pl.BlockSpec((seq, w), lambda b: (b, 0))
    full = lambda a: pl.BlockSpec(a.shape, lambda b: (0,) * a.ndim)
    kern = functools.partial(_retention_kernel, n_chunks=seq // RET_CHUNK, chunk_decay=chunk_decay)
    return pl.pallas_call(
        kern,
        grid=(batch,),
        in_specs=[seq_blk(RET_QK_W), seq_blk(RET_QK_W), seq_blk(RET_V_W), seq_blk(RET_V_W),
                  full(decay), full(xi), full(zeta), full(gnw)],
        out_specs=seq_blk(RET_V_W),
        out_shape=jax.ShapeDtypeStruct((n, RET_V_W), BF16),
        scratch_shapes=[pltpu.VMEM((RET_HEADS, RET_DK, RET_DV), F32)],
        compiler_params=_params(("parallel",)),
        name="retention",
    )(rq, rk, rv, rg, jnp.asarray(decay), jnp.asarray(xi), jnp.asarray(zeta), gnw)


HEADS_PER_STEP = LANES // MOBA_DH


def _moba_kernel(q_ref, k_ref, vt_ref, o_ref, bias_ref, acc_ref, *, n_blocks):
    blk = MOBA_BLOCK
    dh = MOBA_DH
    scale = dh ** -0.5
    heads = range(HEADS_PER_STEP)
    k_means = []
    for hh in heads:
        kf = k_ref[:, hh * dh:(hh + 1) * dh].astype(F32)
        k_means.append(jnp.mean(kf.reshape(n_blocks, blk, dh), axis=1))
    blk_id = lax.broadcasted_iota(jnp.int32, (n_blocks, blk), 0)
    key_pos = lax.broadcasted_iota(jnp.int32, (blk, blk), 0)
    qry_pos = lax.broadcasted_iota(jnp.int32, (blk, blk), 1)
    causal = key_pos <= qry_pos

    def scores_t(hh, key_row0, q):
        kb = k_ref[pl.ds(key_row0, blk), hh * dh:(hh + 1) * dh]
        return lax.dot_general(kb, q, NT_DIMS, preferred_element_type=F32) * scale

    def q_block(j, carry):
        q0 = pl.multiple_of(j * blk, blk)
        qs = [q_ref[pl.ds(q0, blk), hh * dh:(hh + 1) * dh] for hh in heads]
        ms, ls = [], []
        for hh in heads:
            gate = lax.dot_general(k_means[hh], qs[hh].astype(F32), NT_DIMS, preferred_element_type=F32)
            past = blk_id < j
            gate = jnp.where(past, gate, NEG_INF)
            rank = jnp.zeros((n_blocks, blk), jnp.int32)
            for other in range(n_blocks):
                g_o = gate[other:other + 1, :]
                ahead = (g_o > gate) | ((g_o == gate) & (other < blk_id))
                rank = rank + ahead.astype(jnp.int32)
            chosen = past & (rank < MOBA_TOPK)
            bias_ref[hh] = jnp.where(chosen, 0.0, NEG_INF)
            s = jnp.where(causal, scores_t(hh, q0, qs[hh]), NEG_INF)
            m = jnp.max(s, axis=0, keepdims=True)
            p = jnp.exp(s - m)
            ls.append(jnp.sum(p, axis=0, keepdims=True))
            ms.append(m)
            vt = vt_ref[0, hh * dh:(hh + 1) * dh, pl.ds(q0, blk)]
            acc_ref[hh] = jnp.dot(vt, p.astype(BF16), preferred_element_type=F32)

        def k_block(i, ml):
            k0 = pl.multiple_of(i * blk, blk)
            out = []
            for hh in heads:
                m_prev, l_prev = ml[2 * hh], ml[2 * hh + 1]
                s = scores_t(hh, k0, qs[hh]) + bias_ref[hh, pl.ds(i, 1), :]
                m_new = jnp.maximum(m_prev, jnp.max(s, axis=0, keepdims=True))
                alpha = jnp.exp(m_prev - m_new)
                p = jnp.exp(s - m_new)
                l_new = alpha * l_prev + jnp.sum(p, axis=0, keepdims=True)
                vt = vt_ref[0, hh * dh:(hh + 1) * dh, pl.ds(k0, blk)]
                acc_ref[hh] = alpha * acc_ref[hh] + jnp.dot(vt, p.astype(BF16), preferred_element_type=F32)
                out += [m_new, l_new]
            return tuple(out)

        init = []
        for hh in heads:
            init += [ms[hh], ls[hh]]
        ml = lax.fori_loop(0, j, k_block, tuple(init))
        outs = [(acc_ref[hh] / ml[2 * hh + 1]).T for hh in heads]
        o_ref[pl.ds(q0, blk), :] = jnp.concatenate(outs, axis=-1).astype(BF16)
        return carry

    lax.fori_loop(0, n_blocks, q_block, 0)


def _moba(mq, mk, mvt, batch, seq):
    n = mq.shape[0]
    n_blocks = seq // MOBA_BLOCK
    pairs = MOBA_HEADS // HEADS_PER_STEP
    qk_spec = pl.BlockSpec((seq, LANES), lambda b, p: (b, p))
    kern = functools.partial(_moba_kernel, n_blocks=n_blocks)
    return pl.pallas_call(
        kern,
        grid=(batch, pairs),
        in_specs=[qk_spec, qk_spec, pl.BlockSpec((1, LANES, seq), lambda b, p: (b, p, 0))],
        out_specs=qk_spec,
        out_shape=jax.ShapeDtypeStruct((n, MOBA_W), BF16),
        scratch_shapes=[pltpu.VMEM((HEADS_PER_STEP, n_blocks, MOBA_BLOCK), F32),
                        pltpu.VMEM((HEADS_PER_STEP, MOBA_DH, MOBA_BLOCK), F32)],
        compiler_params=_params(("parallel", "parallel")),
        name="moba",
    )(mq, mk, mvt)


def _store_row_tiled(ref, val):
    rows = val.shape[0]
    for s in range(ROW_TILES):
        ref[pl.ds(s, rows, stride=ROW_TILES), :] = val[:, s * LANES:(s + 1) * LANES]


def _load_row_tiled(ref, rows, lead=None):
    parts = []
    for s in range(ROW_TILES):
        idx = (pl.ds(s, rows, stride=ROW_TILES), slice(None))
        parts.append(ref[idx] if lead is None else ref[(lead,) + idx])
    return jnp.concatenate(parts, axis=-1)


def _route_kernel(x_ref, ret_ref, moba_ref, wor_ref, wom_ref, nw_ref, wrt_ref, br_ref, tri_ref,
                  x1_ref, h2_ref, te_ref, gate_ref, rank_ref, cnt_ref, carry_ref):
    @pl.when(pl.program_id(0) == 0)
    def _():
        carry_ref[...] = jnp.zeros_like(carry_ref)

    x1 = (x_ref[...]
          + jnp.dot(ret_ref[...], wor_ref[...], preferred_element_type=F32)
          + jnp.dot(moba_ref[...], wom_ref[...], preferred_element_type=F32))
    x1_ref[...] = x1
    ms = jnp.mean(x1 * x1, axis=-1, keepdims=True)
    h2 = x1 * lax.rsqrt(ms + RMS_EPS) * nw_ref[...]
    _store_row_tiled(h2_ref, h2)
    logits = lax.dot_general(wrt_ref[...], h2, NT_DIMS, precision=lax.Precision.HIGHEST,
                             preferred_element_type=F32) + br_ref[...]
    tm = logits.shape[1]
    e_id = lax.broadcasted_iota(jnp.int32, (N_EXPERTS, tm), 0)
    vals, ids = [], []
    for _ in range(TOP_K):
        m = jnp.max(logits, axis=0, keepdims=True)
        idx = jnp.min(jnp.where(logits == m, e_id, N_EXPERTS), axis=0, keepdims=True)
        vals.append(m)
        ids.append(idx)
        logits = jnp.where(e_id == idx, -jnp.inf, logits)
    exps = [jnp.exp(v - vals[0]) for v in vals]
    denom = exps[0] + exps[1] + exps[2] + exps[3]
    gate_ref[...] = jnp.concatenate([e / denom for e in exps], axis=0)
    te_ref[...] = jnp.concatenate(ids, axis=0)
    hits = [e_id == idx for idx in ids]
    picked = (hits[0] | hits[1] | hits[2] | hits[3])
    picked_f = jnp.where(picked, 1.0, 0.0)
    before = jnp.dot(picked_f.astype(BF16), tri_ref[...], preferred_element_type=F32) + carry_ref[...]
    ranks = [jnp.sum(jnp.where(hit, before, 0.0), axis=0, keepdims=True) for hit in hits]
    rank_ref[...] = jnp.concatenate(ranks, axis=0).astype(jnp.int32)
    carry_ref[...] = carry_ref[...] + jnp.sum(picked_f, axis=1, keepdims=True)
    cnt_ref[...] = jnp.broadcast_to(carry_ref[...], cnt_ref.shape)


def _route(x2, ret, moba, w_out, norm_w, w_router, b_router, tm):
    n = x2.shape[0]
    wo_r = w_out[:RET_V_W].astype(BF16)
    wo_m = w_out[RET_V_W:].astype(BF16)
    nw = norm_w.reshape(1, D_MODEL)
    wrt = w_router.T.astype(F32)
    br = b_router.reshape(N_EXPERTS, 1).astype(F32)
    tri = (np.arange(tm)[:, None] < np.arange(tm)[None, :]).astype(np.float32)
    tri = jnp.asarray(tri, dtype=BF16)
    row = lambda w: pl.BlockSpec((tm, w), lambda i: (i, 0))
    col = pl.BlockSpec((TOP_K, tm), lambda i: (0, i))
    full = lambda a: pl.BlockSpec(a.shape, lambda i: (0,) * a.ndim)
    out_shape = (
        jax.ShapeDtypeStruct((n, D_MODEL), F32),
        jax.ShapeDtypeStruct((n * ROW_TILES, LANES), F32),
        jax.ShapeDtypeStruct((TOP_K, n), jnp.int32),
        jax.ShapeDtypeStruct((TOP_K, n), F32),
        jax.ShapeDtypeStruct((TOP_K, n), jnp.int32),
        jax.ShapeDtypeStruct((N_EXPERTS, LANES), F32),
    )
    return pl.pallas_call(
        _route_kernel,
        grid=(n // tm,),
        in_specs=[row(D_MODEL), row(RET_V_W), row(MOBA_W), full(wo_r), full(wo_m), full(nw), full(wrt),
                  full(br), full(tri)],
        out_specs=(row(D_MODEL), pl.BlockSpec((tm * ROW_TILES, LANES), lambda i: (i, 0)), col, col, col,
                   pl.BlockSpec((N_EXPERTS, LANES), lambda i: (0, 0))),
        out_shape=out_shape,
        scratch_shapes=[pltpu.VMEM((N_EXPERTS, 1), F32)],
        compiler_params=_params(("arbitrary",)),
        name="out_proj_route",
    )(x2, ret, moba, wo_r, wo_m, nw, wrt, br, tri)


def _row_copy(src, src_row, dst, dst_row, sem):
    return pltpu.make_async_copy(
        src.at[pl.ds(pl.multiple_of(src_row * ROW_TILES, ROW_TILES), ROW_TILES)],
        dst.at[pl.ds(pl.multiple_of(dst_row * ROW_TILES, ROW_TILES), ROW_TILES)], sem)


def _dispatch_kernel(fill_lo_ref, fill_n_ref, dest_ref, h2_ref, xs_ref, zero_ref, sem, fill_sem, *, tt):
    tile = pl.program_id(0)

    @pl.when(tile == 0)
    def _():
        zero_ref[...] = jnp.zeros_like(zero_ref)

        def fill(e, carry):
            def one(r, c):
                cp = pltpu.make_async_copy(
                    zero_ref,
                    xs_ref.at[pl.ds(pl.multiple_of((fill_lo_ref[e] + r) * ROW_TILES, ROW_TILES), ROW_TILES)],
                    fill_sem)
                cp.start()
                cp.wait()
                return c
            return lax.fori_loop(0, fill_n_ref[e], one, carry)

        lax.fori_loop(0, N_EXPERTS, fill, 0)

    def issue(t, carry):
        for k in range(TOP_K):
            _row_copy(h2_ref, tile * tt + t, xs_ref, dest_ref[0, 0, k * tt + t], sem).start()
        return carry

    lax.fori_loop(0, tt, issue, 0)

    def drain(t, carry):
        for k in range(TOP_K):
            _row_copy(h2_ref, 0, xs_ref, 0, sem).wait()
        return carry

    lax.fori_loop(0, tt, drain, 0)


def _dispatch(h2_tiled, dest, fill_lo, fill_n, n_slots, tt):
    n = dest.shape[1]
    tiles = n // tt
    dest_t = dest.reshape(TOP_K, tiles, tt).transpose(1, 0, 2).reshape(tiles, 1, TOP_K * tt)
    grid_spec = pltpu.PrefetchScalarGridSpec(
        num_scalar_prefetch=2,
        grid=(tiles,),
        in_specs=[pl.BlockSpec((1, 1, TOP_K * tt), lambda i, lo, cnt: (i, 0, 0), memory_space=pltpu.SMEM),
                  pl.BlockSpec(memory_space=pl.ANY)],
        out_specs=pl.BlockSpec(memory_space=pl.ANY),
        scratch_shapes=[pltpu.VMEM((ROW_TILES, LANES), F32), pltpu.SemaphoreType.DMA(()),
                        pltpu.SemaphoreType.DMA(())],
    )
    return pl.pallas_call(
        functools.partial(_dispatch_kernel, tt=tt),
        grid_spec=grid_spec,
        out_shape=jax.ShapeDtypeStruct((n_slots * ROW_TILES, LANES), F32),
        compiler_params=_params(("arbitrary",)),
        name="dispatch",
    )(fill_lo, fill_n, dest_t, h2_tiled)


def _expert_kernel(be_ref, used_ref, xs_ref, w1_ref, b1_ref, w2_ref, b2_ref, y_ref, *, mb):
    b = pl.program_id(0)

    @pl.when(b < used_ref[0])
    def _():
        x = _load_row_tiled(xs_ref, mb).astype(BF16)
        u = jnp.dot(x, w1_ref[0], preferred_element_type=F32) + b1_ref[0]
        gate = jnp.minimum(u[:, :D_EXPERT], SWIGLU_LIMIT)
        lin = jnp.clip(u[:, D_EXPERT:], -SWIGLU_LIMIT, SWIGLU_LIMIT)
        act = gate * jax.nn.sigmoid(SWIGLU_ALPHA * gate) * (lin + 1.0)
        y = jnp.dot(act.astype(BF16), w2_ref[0], preferred_element_type=F32) + b2_ref[0]
        _store_row_tiled(y_ref, y)

    @pl.when(b >= used_ref[0])
    def _():
        y_ref[...] = jnp.zeros_like(y_ref)


def _experts(xs, block_expert, n_used, w1, b1, w2, b2, mb):
    n_blocks = block_expert.shape[0]
    w1b = w1.astype(BF16)
    w2b = w2.astype(BF16)
    b1r = b1.reshape(N_EXPERTS, 1, 2 * D_EXPERT).astype(F32)
    b2r = b2.reshape(N_EXPERTS, 1, D_MODEL).astype(F32)
    grid_spec = pltpu.PrefetchScalarGridSpec(
        num_scalar_prefetch=2,
        grid=(n_blocks,),
        in_specs=[
            pl.BlockSpec((mb * ROW_TILES, LANES), lambda i, be, used: (jnp.minimum(i, used[0] - 1), 0)),
            pl.BlockSpec((1, D_MODEL, 2 * D_EXPERT), lambda i, be, used: (be[i], 0, 0)),
            pl.BlockSpec((1, 1, 2 * D_EXPERT), lambda i, be, used: (be[i], 0, 0)),
            pl.BlockSpec((1, D_EXPERT, D_MODEL), lambda i, be, used: (be[i], 0, 0)),
            pl.BlockSpec((1, 1, D_MODEL), lambda i, be, used: (be[i], 0, 0)),
        ],
        out_specs=pl.BlockSpec((mb * ROW_TILES, LANES), lambda i, be, used: (i, 0)),
    )
    return pl.pallas_call(
        functools.partial(_expert_kernel, mb=mb),
        grid_spec=grid_spec,
        out_shape=jax.ShapeDtypeStruct(xs.shape, F32),
        compiler_params=_params(("arbitrary",)),
        name="experts",
    )(block_expert, n_used, xs, w1b, b1r, w2b, b2r)


def _combine_kernel(dest_ref, x1_ref, gate_ref, fw_ref, y_ref, o_ref, buf_ref, sem, *, tc):
    def issue(t, carry):
        for k in range(TOP_K):
            pltpu.make_async_copy(
                y_ref.at[pl.ds(pl.multiple_of(dest_ref[0, 0, k * tc + t] * ROW_TILES, ROW_TILES), ROW_TILES)],
                buf_ref.at[k, pl.ds(pl.multiple_of(t * ROW_TILES, ROW_TILES), ROW_TILES)], sem).start()
        return carry

    lax.fori_loop(0, tc, issue, 0)

    def drain(t, carry):
        for k in range(TOP_K):
            pltpu.make_async_copy(y_ref.at[pl.ds(0, ROW_TILES)], buf_ref.at[k, pl.ds(0, ROW_TILES)], sem).wait()
        return carry

    lax.fori_loop(0, tc, drain, 0)

    gates = gate_ref[...]
    gates_t = jnp.concatenate([gates, jnp.zeros((SUBLANES - TOP_K, tc), F32)], axis=0).T
    moe = None
    for k in range(TOP_K):
        part = gates_t[:, k:k + 1] * _load_row_tiled(buf_ref, tc, lead=k)
        moe = part if moe is None else moe + part
    xo = x1_ref[...] + moe
    ms = jnp.mean(xo * xo, axis=-1, keepdims=True)
    o_ref[...] = xo * lax.rsqrt(ms + RMS_EPS) * fw_ref[...]


def _combine(x1, gates, dest, y, final_w, tc):
    n = x1.shape[0]
    tiles = n // tc
    dest_t = dest.reshape(TOP_K, tiles, tc).transpose(1, 0, 2).reshape(tiles, 1, TOP_K * tc)
    fw = final_w.reshape(1, D_MODEL)
    return pl.pallas_call(
        functools.partial(_combine_kernel, tc=tc),
        grid=(tiles,),
        in_specs=[pl.BlockSpec((1, 1, TOP_K * tc), lambda i: (i, 0, 0), memory_space=pltpu.SMEM),
                  pl.BlockSpec((tc, D_MODEL), lambda i: (i, 0)),
                  pl.BlockSpec((TOP_K, tc), lambda i: (0, i)),
                  pl.BlockSpec((1, D_MODEL), lambda i: (0, 0)),
                  pl.BlockSpec(memory_space=pl.ANY)],
        out_specs=pl.BlockSpec((tc, D_MODEL), lambda i: (i, 0)),
        out_shape=jax.ShapeDtypeStruct((n, D_MODEL), F32),
        scratch_shapes=[pltpu.VMEM((TOP_K, tc * ROW_TILES, LANES), F32), pltpu.SemaphoreType.DMA(())],
        compiler_params=_params(("arbitrary",)),
        name="combine",
    )(dest_t, x1, gates, fw, y)


PROJ_ROWS = 512
EXPERT_ROWS = 256
DISPATCH_ROWS = 2048
COMBINE_ROWS = 256


def _slot_layout(counts, n_pairs, mb):
    padded = (counts + mb - 1) // mb * mb
    pad_end = jnp.cumsum(padded)
    pad_start = pad_end - padded
    n_blocks = n_pairs // mb + N_EXPERTS
    block_expert = jnp.minimum(
        jnp.searchsorted(pad_end, jnp.arange(n_blocks, dtype=jnp.int32) * mb, side="right"),
        N_EXPERTS - 1).astype(jnp.int32)
    n_used = (pad_end[-1:] // mb).astype(jnp.int32)
    return pad_start.astype(jnp.int32), (pad_start + counts).astype(jnp.int32), (padded - counts).astype(jnp.int32), \
        block_expert, n_used, n_blocks


def _layer(x2, batch, seq, norm1_w, w_in, ret_norm_w, w_out, norm2_w, w_router, b_router, w1, b1, w2, b2,
           final_w):
    n = x2.shape[0]
    tm = min(PROJ_ROWS, seq)
    rq, rk, rv, rg, mq, mk, mvt = _in_projection(x2, norm1_w, w_in, batch, seq, tm)
    ret = _retention(rq, rk, rv, rg, ret_norm_w, batch, seq)
    moba = _moba(mq, mk, mvt, batch, seq)
    x1, h2_tiled, top_e, gates, rank, cnt = _route(x2, ret, moba, w_out, norm2_w, w_router, b_router, tm)
    counts = cnt[:, 0].astype(jnp.int32)
    mb = EXPERT_ROWS
    pad_start, fill_lo, fill_n, block_expert, n_used, n_blocks = _slot_layout(counts, n * TOP_K, mb)
    dest = pad_start[top_e] + rank
    xs = _dispatch(h2_tiled, dest, fill_lo, fill_n, n_blocks * mb, min(DISPATCH_ROWS, n))
    y = _experts(xs, block_expert, n_used, w1, b1, w2, b2, mb)
    return _combine(x1, gates, dest, y, final_w, min(COMBINE_ROWS, n))


def kernel(x, norm1_w, w_in, ret_norm_w, w_out, norm2_w, w_router, b_router, w1, b1, w2, b2, final_norm_w):
    batch, seq, d = x.shape
    depth = w_in.shape[0]
    assert depth == 1 and d == D_MODEL and seq % MOBA_BLOCK == 0
    x2 = x.reshape(batch * seq, d)
    out = _layer(x2, batch, seq, norm1_w[0], w_in[0], ret_norm_w[0], w_out[0], norm2_w[0], w_router[0],
                 b_router[0], w1[0], b1[0], w2[0], b2[0], final_norm_w)
    return out.reshape(batch, seq, d)
```

```python
import functools

import numpy as np
import jax
import jax.numpy as jnp
from jax import lax
from jax.experimental import pallas as pl
from jax.experimental.pallas import tpu as pltpu

D_MODEL = 1024
RET_HEADS = 4
RET_DK = 64
RET_DV = 128
RET_CHUNK = 128
RET_THETA = 10000.0
MOBA_HEADS = 8
MOBA_DH = 64
MOBA_BLOCK = 256
MOBA_TOPK = 3
ROPE_THETA = 500000.0
ROPE_DIM = MOBA_DH // 4
RET_QK_W = RET_HEADS * RET_DK
RET_V_W = RET_HEADS * RET_DV
MOBA_W = MOBA_HEADS * MOBA_DH
N_EXPERTS = 32
TOP_K = 4
D_EXPERT = D_MODEL
SWIGLU_LIMIT = 7.0
SWIGLU_ALPHA = 1.702
RMS_EPS = 1e-5
GN_EPS = 1e-6
NEG_INF = -1e30

SUBLANES = 8
LANES = 128
ROW_TILES = D_MODEL // LANES
VMEM_LIMIT = 48 * 1024 * 1024

F32 = jnp.float32
BF16 = jnp.bfloat16
NT_DIMS = (((1,), (1,)), ((), ()))
TN_DIMS = (((0,), (0,)), ((), ()))


def _params(sem, vmem=VMEM_LIMIT):
    return pltpu.CompilerParams(dimension_semantics=sem, vmem_limit_bytes=vmem)


def _rotate(x, cos, sin_lo, sin_hi, half):
    width = x.shape[-1]
    nxt = pltpu.roll(x, width - half, axis=1)
    prv = pltpu.roll(x, half, axis=1)
    return x * cos + nxt * sin_lo + prv * sin_hi


def _inproj_kernel(x_ref, nw_ref, w_ref, wvt_ref, rc_ref, rlo_ref, rhi_ref, mc_ref, mlo_ref, mhi_ref,
                   rq_ref, rk_ref, rv_ref, rg_ref, mq_ref, mk_ref, mvt_ref):
    x = x_ref[...]
    ms = jnp.mean(x * x, axis=-1, keepdims=True)
    h = (x * lax.rsqrt(ms + RMS_EPS) * nw_ref[...]).astype(BF16)
    proj = jnp.dot(h, w_ref[...], preferred_element_type=F32)
    o = 0
    rq = proj[:, o:o + RET_QK_W]; o += RET_QK_W
    rk = proj[:, o:o + RET_QK_W]; o += RET_QK_W
    rv = proj[:, o:o + RET_V_W]; o += RET_V_W
    rg = proj[:, o:o + RET_V_W]; o += RET_V_W
    mq = proj[:, o:o + MOBA_W]; o += MOBA_W
    mk = proj[:, o:o + MOBA_W]
    rc, rlo, rhi = rc_ref[...], rlo_ref[...], rhi_ref[...]
    mc, mlo, mhi = mc_ref[...], mlo_ref[...], mhi_ref[...]
    rq_ref[...] = _rotate(rq, rc, rlo, rhi, RET_DK // 2).astype(BF16)
    rk_ref[...] = _rotate(rk, rc, rlo, rhi, RET_DK // 2).astype(BF16)
    rv_ref[...] = rv.astype(BF16)
    rg_ref[...] = rg.astype(BF16)
    mq_ref[...] = _rotate(mq, mc, mlo, mhi, ROPE_DIM // 2).astype(BF16)
    mk_ref[...] = _rotate(mk, mc, mlo, mhi, ROPE_DIM // 2).astype(BF16)
    mvt_ref[0] = lax.dot_general(wvt_ref[...], h, NT_DIMS, preferred_element_type=F32).astype(BF16)


def _rotary_tables(seq, rot_dim, theta, head_dim, heads):
    half = rot_dim // 2
    inv_freq = (np.float32(theta) ** (np.float32(-2.0) * np.arange(half, dtype=np.float32) / np.float32(rot_dim)))
    ang = np.arange(seq, dtype=np.float32)[:, None] * inv_freq.astype(np.float32)[None, :]
    cos, sin = np.cos(ang).astype(np.float32), np.sin(ang).astype(np.float32)
    rest = head_dim - rot_dim
    ones = np.ones((seq, rest), np.float32)
    zeros = np.zeros((seq, rest), np.float32)
    zh = np.zeros((seq, half), np.float32)
    cos_t = np.concatenate([cos, cos, ones], axis=-1)
    lo_t = np.concatenate([-sin, zh, zeros], axis=-1)
    hi_t = np.concatenate([zh, sin, zeros], axis=-1)
    tile = lambda t: jnp.asarray(np.tile(t, (1, heads)))
    return tile(cos_t), tile(lo_t), tile(hi_t)


def _in_projection(x2, norm_w, w_in, batch, seq, tm):
    n = x2.shape[0]
    st = seq // tm
    w_main = w_in[:, :D_MODEL * 3 - MOBA_W].astype(BF16)
    w_vt = w_in[:, D_MODEL * 3 - MOBA_W:].T.astype(BF16)
    wm = w_main.shape[1]
    r_tabs = _rotary_tables(seq, RET_DK, RET_THETA, RET_DK, RET_HEADS)
    m_tabs = _rotary_tables(seq, ROPE_DIM, ROPE_THETA, MOBA_DH, MOBA_HEADS)
    row = lambda w: pl.BlockSpec((tm, w), lambda b, s: (b * st + s, 0))
    pos = lambda w: pl.BlockSpec((tm, w), lambda b, s: (s, 0))
    full = lambda a: pl.BlockSpec(a.shape, lambda b, s: (0,) * a.ndim)
    nw = norm_w.reshape(1, D_MODEL)
    out_shape = (
        jax.ShapeDtypeStruct((n, RET_QK_W), BF16), jax.ShapeDtypeStruct((n, RET_QK_W), BF16),
        jax.ShapeDtypeStruct((n, RET_V_W), BF16), jax.ShapeDtypeStruct((n, RET_V_W), BF16),
        jax.ShapeDtypeStruct((n, MOBA_W), BF16), jax.ShapeDtypeStruct((n, MOBA_W), BF16),
        jax.ShapeDtypeStruct((batch, MOBA_W, seq), BF16),
    )
    return pl.pallas_call(
        _inproj_kernel,
        grid=(batch, st),
        in_specs=[row(D_MODEL), full(nw), full(w_main), full(w_vt),
                  pos(RET_QK_W), pos(RET_QK_W), pos(RET_QK_W), pos(MOBA_W), pos(MOBA_W), pos(MOBA_W)],
        out_specs=(row(RET_QK_W), row(RET_QK_W), row(RET_V_W), row(RET_V_W), row(MOBA_W), row(MOBA_W),
                   pl.BlockSpec((1, MOBA_W, tm), lambda b, s: (b, 0, s))),
        out_shape=out_shape,
        compiler_params=_params(("parallel", "parallel")),
        name="in_projection",
    )(x2, nw, w_main, w_vt, *r_tabs, *m_tabs)


def _retention_kernel(q_ref, k_ref, v_ref, g_ref, decay_ref, xi_ref, zeta_ref, gnw_ref, o_ref, state_ref,
                      *, n_chunks, chunk_decay):
    c = RET_CHUNK
    state_ref[...] = jnp.zeros_like(state_ref)

    def chunk(ci, carry):
        r0 = pl.multiple_of(ci * c, c)
        q = q_ref[pl.ds(r0, c), :]
        k = k_ref[pl.ds(r0, c), :]
        v = v_ref[pl.ds(r0, c), :]
        g = g_ref[pl.ds(r0, c), :].astype(F32)
        qx = (q.astype(F32) * xi_ref[...]).astype(BF16)
        kz = (k.astype(F32) * zeta_ref[...]).astype(BF16)
        outs = []
        for h in range(RET_HEADS):
            qs = slice(h * RET_DK, (h + 1) * RET_DK)
            vs = slice(h * RET_DV, (h + 1) * RET_DV)
            vh = v[:, vs]
            scores = lax.dot_general(q[:, qs], k[:, qs], NT_DIMS, preferred_element_type=F32) * decay_ref[h]
            inner = jnp.dot(scores.astype(BF16), vh, preferred_element_type=F32)
            state = state_ref[h]
            cross = jnp.dot(qx[:, qs], state.astype(BF16), preferred_element_type=F32)
            state_ref[h] = chunk_decay[h] * state + lax.dot_general(
                kz[:, qs], vh, TN_DIMS, preferred_element_type=F32)
            ret = inner + cross
            mu = jnp.mean(ret, axis=-1, keepdims=True)
            cen = ret - mu
            var = jnp.mean(cen * cen, axis=-1, keepdims=True)
            outs.append(cen * lax.rsqrt(var + GN_EPS))
        normed = jnp.concatenate(outs, axis=-1) * gnw_ref[...]
        o_ref[pl.ds(r0, c), :] = (g * jax.nn.sigmoid(g) * normed).astype(BF16)
        return carry

    lax.fori_loop(0, n_chunks, chunk, 0)


def _retention_consts():
    h = np.arange(RET_HEADS, dtype=np.float32)
    log_g = np.log1p(-np.exp2(-5.0 - h)).astype(np.float32)
    idx = np.arange(RET_CHUNK, dtype=np.float32)
    rel = idx[:, None] - idx[None, :]
    scale = np.float32(RET_DK ** -0.5)
    decay = np.where(rel >= 0, np.exp(log_g[:, None, None] * np.maximum(rel, 0.0)), 0.0).astype(np.float32) * scale
    zeta = np.exp(log_g[:, None] * (RET_CHUNK - 1.0 - idx)).astype(np.float32) * scale
    xi = np.exp(log_g[:, None] * (idx + 1.0)).astype(np.float32)
    spread = lambda t: np.repeat(t.T, RET_DK, axis=1)
    chunk_decay = tuple(float(v) for v in np.exp(log_g * RET_CHUNK).astype(np.float32))
    return decay, spread(xi), spread(zeta), chunk_decay


def _retention(rq, rk, rv, rg, gn_w, batch, seq):
    n = rq.shape[0]
    decay, xi, zeta, chunk_decay = _retention_consts()
    gnw = gn_w.reshape(1, RET_V_W).astype(F32)
    seq_blk = lambda w: pl.BlockSpec((seq, w), lambda b: (b, 0))
    full = lambda a: pl.BlockSpec(a.shape, lambda b: (0,) * a.ndim)
    kern = functools.partial(_retention_kernel, n_chunks=seq // RET_CHUNK, chunk_decay=chunk_decay)
    return pl.pallas_call(
        kern,
        grid=(batch,),
        in_specs=[seq_blk(RET_QK_W), seq_blk(RET_QK_W), seq_blk(RET_V_W), seq_blk(RET_V_W),
                  full(decay), full(xi), full(zeta), full(gnw)],
        out_specs=seq_blk(RET_V_W),
        out_shape=jax.ShapeDtypeStruct((n, RET_V_W), BF16),
        scratch_shapes=[pltpu.VMEM((RET_HEADS, RET_DK, RET_DV), F32)],
        compiler_params=_params(("parallel",)),
        name="retention",
    )(rq, rk, rv, rg, jnp.asarray(decay), jnp.asarray(xi), jnp.asarray(zeta), gnw)


HEADS_PER_STEP = LANES // MOBA_DH


def _moba_kernel(q_ref, k_ref, vt_ref, o_ref, bias_ref, acc_ref, *, n_blocks):
    blk = MOBA_BLOCK
    dh = MOBA_DH
    scale = dh ** -0.5
    heads = range(HEADS_PER_STEP)
    k_means = []
    for hh in heads:
        kf = k_ref[:, hh * dh:(hh + 1) * dh].astype(F32)
        k_means.append(jnp.mean(kf.reshape(n_blocks, blk, dh), axis=1))
    blk_id = lax.broadcasted_iota(jnp.int32, (n_blocks, blk), 0)
    key_pos = lax.broadcasted_iota(jnp.int32, (blk, blk), 0)
    qry_pos = lax.broadcasted_iota(jnp.int32, (blk, blk), 1)
    causal = key_pos <= qry_pos

    def scores_t(hh, key_row0, q):
        kb = k_ref[pl.ds(key_row0, blk), hh * dh:(hh + 1) * dh]
        return lax.dot_general(kb, q, NT_DIMS, preferred_element_type=F32) * scale

    def q_block(j, carry):
        q0 = pl.multiple_of(j * blk, blk)
        qs = [q_ref[pl.ds(q0, blk), hh * dh:(hh + 1) * dh] for hh in heads]
        ms, ls = [], []
        for hh in heads:
            gate = lax.dot_general(k_means[hh], qs[hh].astype(F32), NT_DIMS, preferred_element_type=F32)
            past = blk_id < j
            gate = jnp.where(past, gate, NEG_INF)
            rank = jnp.zeros((n_blocks, blk), jnp.int32)
            for other in range(n_blocks):
                g_o = gate[other:other + 1, :]
                ahead = (g_o > gate) | ((g_o == gate) & (other < blk_id))
                rank = rank + ahead.astype(jnp.int32)
            chosen = past & (rank < MOBA_TOPK)
            bias_ref[hh] = jnp.where(chosen, 0.0, NEG_INF)
            s = jnp.where(causal, scores_t(hh, q0, qs[hh]), NEG_INF)
            m = jnp.max(s, axis=0, keepdims=True)
            p = jnp.exp(s - m)
            ls.append(jnp.sum(p, axis=0, keepdims=True))
            ms.append(m)
            vt = vt_ref[0, hh * dh:(hh + 1) * dh, pl.ds(q0, blk)]
            acc_ref[hh] = jnp.dot(vt, p.astype(BF16), preferred_element_type=F32)

        def k_block(i, ml):
            k0 = pl.multiple_of(i * blk, blk)
            out = []
            for hh in heads:
                m_prev, l_prev = ml[2 * hh], ml[2 * hh + 1]
                s = scores_t(hh, k0, qs[hh]) + bias_ref[hh, pl.ds(i, 1), :]
                m_new = jnp.maximum(m_prev, jnp.max(s, axis=0, keepdims=True))
                alpha = jnp.exp(m_prev - m_new)
                p = jnp.exp(s - m_new)
                l_new = alpha * l_prev + jnp.sum(p, axis=0, keepdims=True)
                vt = vt_ref[0, hh * dh:(hh + 1) * dh, pl.ds(k0, blk)]
                acc_ref[hh] = alpha * acc_ref[hh] + jnp.dot(vt, p.astype(BF16), preferred_element_type=F32)
                out += [m_new, l_new]
            return tuple(out)

        init = []
        for hh in heads:
            init += [ms[hh], ls[hh]]
        ml = lax.fori_loop(0, j, k_block, tuple(init))
        outs = [(acc_ref[hh] / ml[2 * hh + 1]).T for hh in heads]
        o_ref[pl.ds(q0, blk), :] = jnp.concatenate(outs, axis=-1).astype(BF16)
        return carry

    lax.fori_loop(0, n_blocks, q_block, 0)


def _moba(mq, mk, mvt, batch, seq):
    n = mq.shape[0]
    n_blocks = seq // MOBA_BLOCK
    pairs = MOBA_HEADS // HEADS_PER_STEP
    qk_spec = pl.BlockSpec((seq, LANES), lambda b, p: (b, p))
    kern = functools.partial(_moba_kernel, n_blocks=n_blocks)
    return pl.pallas_call(
        kern,
        grid=(batch, pairs),
        in_specs=[qk_spec, qk_spec, pl.BlockSpec((1, LANES, seq), lambda b, p: (b, p, 0))],
        out_specs=qk_spec,
        out_shape=jax.ShapeDtypeStruct((n, MOBA_W), BF16),
        scratch_shapes=[pltpu.VMEM((HEADS_PER_STEP, n_blocks, MOBA_BLOCK), F32),
                        pltpu.VMEM((HEADS_PER_STEP, MOBA_DH, MOBA_BLOCK), F32)],
        compiler_params=_params(("parallel", "parallel")),
        name="moba",
    )(mq, mk, mvt)


def _store_row_tiled(ref, val):
    rows = val.shape[0]
    for s in range(ROW_TILES):
        ref[pl.ds(s, rows, stride=ROW_TILES), :] = val[:, s * LANES:(s + 1) * LANES]


def _load_row_tiled(ref, rows, lead=None):
    parts = []
    for s in range(ROW_TILES):
        idx = (pl.ds(s, rows, stride=ROW_TILES), slice(None))
        parts.append(ref[idx] if lead is None else ref[(lead,) + idx])
    return jnp.concatenate(parts, axis=-1)


def _route_kernel(x_ref, ret_ref, moba_ref, wor_ref, wom_ref, nw_ref, wrt_ref, br_ref, tri_ref,
                  x1_ref, h2_ref, te_ref, gate_ref, rank_ref, cnt_ref, carry_ref):
    @pl.when(pl.program_id(0) == 0)
    def _():
        carry_ref[...] = jnp.zeros_like(carry_ref)

    x1 = (x_ref[...]
          + jnp.dot(ret_ref[...], wor_ref[...], preferred_element_type=F32)
          + jnp.dot(moba_ref[...], wom_ref[...], preferred_element_type=F32))
    x1_ref[...] = x1
    ms = jnp.mean(x1 * x1, axis=-1, keepdims=True)
    h2 = x1 * lax.rsqrt(ms + RMS_EPS) * nw_ref[...]
    _store_row_tiled(h2_ref, h2)
    logits = lax.dot_general(wrt_ref[...], h2, NT_DIMS, precision=lax.Precision.HIGHEST,
                             preferred_element_type=F32) + br_ref[...]
    tm = logits.shape[1]
    e_id = lax.broadcasted_iota(jnp.int32, (N_EXPERTS, tm), 0)
    vals, ids = [], []
    for _ in range(TOP_K):
        m = jnp.max(logits, axis=0, keepdims=True)
        idx = jnp.min(jnp.where(logits == m, e_id, N_EXPERTS), axis=0, keepdims=True)
        vals.append(m)
        ids.append(idx)
        logits = jnp.where(e_id == idx, -jnp.inf, logits)
    exps = [jnp.exp(v - vals[0]) for v in vals]
    denom = exps[0] + exps[1] + exps[2] + exps[3]
    gate_ref[...] = jnp.concatenate([e / denom for e in exps], axis=0)
    te_ref[...] = jnp.concatenate(ids, axis=0)
    hits = [e_id == idx for idx in ids]
    picked = (hits[0] | hits[1] | hits[2] | hits[3])
    picked_f = jnp.where(picked, 1.0, 0.0)
    before = jnp.dot(picked_f.astype(BF16), tri_ref[...], preferred_element_type=F32) + carry_ref[...]
    ranks = [jnp.sum(jnp.where(hit, before, 0.0), axis=0, keepdims=True) for hit in hits]
    rank_ref[...] = jnp.concatenate(ranks, axis=0).astype(jnp.int32)
    carry_ref[...] = carry_ref[...] + jnp.sum(picked_f, axis=1, keepdims=True)
    cnt_ref[...] = jnp.broadcast_to(carry_ref[...], cnt_ref.shape)


def _route(x2, ret, moba, w_out, norm_w, w_router, b_router, tm):
    n = x2.shape[0]
    wo_r = w_out[:RET_V_W].astype(BF16)
    wo_m = w_out[RET_V_W:].astype(BF16)
    nw = norm_w.reshape(1, D_MODEL)
    wrt = w_router.T.astype(F32)
    br = b_router.reshape(N_EXPERTS, 1).astype(F32)
    tri = (np.arange(tm)[:, None] < np.arange(tm)[None, :]).astype(np.float32)
    tri = jnp.asarray(tri, dtype=BF16)
    row = lambda w: pl.BlockSpec((tm, w), lambda i: (i, 0))
    col = pl.BlockSpec((TOP_K, tm), lambda i: (0, i))
    full = lambda a: pl.BlockSpec(a.shape, lambda i: (0,) * a.ndim)
    out_shape = (
        jax.ShapeDtypeStruct((n, D_MODEL), F32),
        jax.ShapeDtypeStruct((n * ROW_TILES, LANES), F32),
        jax.ShapeDtypeStruct((TOP_K, n), jnp.int32),
        jax.ShapeDtypeStruct((TOP_K, n), F32),
        jax.ShapeDtypeStruct((TOP_K, n), jnp.int32),
        jax.ShapeDtypeStruct((N_EXPERTS, LANES), F32),
    )
    return pl.pallas_call(
        _route_kernel,
        grid=(n // tm,),
        in_specs=[row(D_MODEL), row(RET_V_W), row(MOBA_W), full(wo_r), full(wo_m), full(nw), full(wrt),
                  full(br), full(tri)],
        out_specs=(row(D_MODEL), pl.BlockSpec((tm * ROW_TILES, LANES), lambda i: (i, 0)), col, col, col,
                   pl.BlockSpec((N_EXPERTS, LANES), lambda i: (0, 0))),
        out_shape=out_shape,
        scratch_shapes=[pltpu.VMEM((N_EXPERTS, 1), F32)],
        compiler_params=_params(("arbitrary",)),
        name="out_proj_route",
    )(x2, ret, moba, wo_r, wo_m, nw, wrt, br, tri)


def _row_slice(ref, row):
    return ref.at[pl.ds(pl.multiple_of(row * ROW_TILES, ROW_TILES), ROW_TILES)]


DISPATCH_UNROLL = 4


def _dispatch_kernel(fill_lo_ref, fill_n_ref, dest_ref, h2_ref, xs_ref, zero_ref, sem, fill_sem, *, tt):
    @pl.when(pl.program_id(0) == 0)
    def _():
        zero_ref[...] = jnp.zeros_like(zero_ref)

        def fill_copy(e, r):
            return pltpu.make_async_copy(zero_ref, _row_slice(xs_ref, fill_lo_ref[e] + r), fill_sem)

        def fill(e, carry):
            def one(r, c):
                fill_copy(e, r).start()
                return c
            return lax.fori_loop(0, fill_n_ref[e], one, carry)

        def fill_wait(e, carry):
            def one(r, c):
                fill_copy(e, r).wait()
                return c
            return lax.fori_loop(0, fill_n_ref[e], one, carry)

        lax.fori_loop(0, N_EXPERTS, fill, 0)
        lax.fori_loop(0, N_EXPERTS, fill_wait, 0)

    def row_copy(t, k):
        return pltpu.make_async_copy(_row_slice(h2_ref, t), _row_slice(xs_ref, dest_ref[0, 0, k * tt + t]), sem)

    def issue(g, carry):
        for u in range(DISPATCH_UNROLL):
            for k in range(TOP_K):
                row_copy(g * DISPATCH_UNROLL + u, k).start()
        return carry

    lax.fori_loop(0, tt // DISPATCH_UNROLL, issue, 0)

    def drain(g, carry):
        for u in range(DISPATCH_UNROLL):
            for k in range(TOP_K):
                row_copy(g * DISPATCH_UNROLL + u, k).wait()
        return carry

    lax.fori_loop(0, tt // DISPATCH_UNROLL, drain, 0)


def _dispatch(h2_tiled, dest, fill_lo, fill_n, n_slots, tt):
    n = dest.shape[1]
    tiles = n // tt
    dest_t = dest.reshape(TOP_K, tiles, tt).transpose(1, 0, 2).reshape(tiles, 1, TOP_K * tt)
    grid_spec = pltpu.PrefetchScalarGridSpec(
        num_scalar_prefetch=2,
        grid=(tiles,),
        in_specs=[pl.BlockSpec((1, 1, TOP_K * tt), lambda i, lo, cnt: (i, 0, 0), memory_space=pltpu.SMEM),
                  pl.BlockSpec((tt * ROW_TILES, LANES), lambda i, lo, cnt: (i, 0))],
        out_specs=pl.BlockSpec(memory_space=pl.ANY),
        scratch_shapes=[pltpu.VMEM((ROW_TILES, LANES), F32), pltpu.SemaphoreType.DMA(()),
                        pltpu.SemaphoreType.DMA(())],
    )
    return pl.pallas_call(
        functools.partial(_dispatch_kernel, tt=tt),
        grid_spec=grid_spec,
        out_shape=jax.ShapeDtypeStruct((n_slots * ROW_TILES, LANES), F32),
        compiler_params=_params(("arbitrary",)),
        name="dispatch",
    )(fill_lo, fill_n, dest_t, h2_tiled)


def _expert_kernel(be_ref, used_ref, xs_ref, w1_ref, b1_ref, w2_ref, b2_ref, y_ref, *, mb):
    b = pl.program_id(0)

    @pl.when(b < used_ref[0])
    def _():
        x = _load_row_tiled(xs_ref, mb).astype(BF16)
        u = jnp.dot(x, w1_ref[0], preferred_element_type=F32) + b1_ref[0]
        gate = jnp.minimum(u[:, :D_EXPERT], SWIGLU_LIMIT)
        lin = jnp.clip(u[:, D_EXPERT:], -SWIGLU_LIMIT, SWIGLU_LIMIT)
        act = gate * jax.nn.sigmoid(SWIGLU_ALPHA * gate) * (lin + 1.0)
        y = jnp.dot(act.astype(BF16), w2_ref[0], preferred_element_type=F32) + b2_ref[0]
        _store_row_tiled(y_ref, y)

    @pl.when(b >= used_ref[0])
    def _():
        y_ref[...] = jnp.zeros_like(y_ref)


def _experts(xs, block_expert, n_used, w1, b1, w2, b2, mb):
    n_blocks = block_expert.shape[0]
    w1b = w1.astype(BF16)
    w2b = w2.astype(BF16)
    b1r = b1.reshape(N_EXPERTS, 1, 2 * D_EXPERT).astype(F32)
    b2r = b2.reshape(N_EXPERTS, 1, D_MODEL).astype(F32)
    grid_spec = pltpu.PrefetchScalarGridSpec(
        num_scalar_prefetch=2,
        grid=(n_blocks,),
        in_specs=[
            pl.BlockSpec((mb * ROW_TILES, LANES), lambda i, be, used: (jnp.minimum(i, used[0] - 1), 0)),
            pl.BlockSpec((1, D_MODEL, 2 * D_EXPERT), lambda i, be, used: (be[i], 0, 0)),
            pl.BlockSpec((1, 1, 2 * D_EXPERT), lambda i, be, used: (be[i], 0, 0)),
            pl.BlockSpec((1, D_EXPERT, D_MODEL), lambda i, be, used: (be[i], 0, 0)),
            pl.BlockSpec((1, 1, D_MODEL), lambda i, be, used: (be[i], 0, 0)),
        ],
        out_specs=pl.BlockSpec((mb * ROW_TILES, LANES), lambda i, be, used: (i, 0)),
    )
    return pl.pallas_call(
        functools.partial(_expert_kernel, mb=mb),
        grid_spec=grid_spec,
        out_shape=jax.ShapeDtypeStruct(xs.shape, F32),
        compiler_params=_params(("arbitrary",)),
        name="experts",
    )(block_expert, n_used, xs, w1b, b1r, w2b, b2r)


def _combine_kernel(dest_ref, x1_ref, gate_ref, fw_ref, y_ref, o_ref, buf_ref, sem, *, tc):
    def issue(t, carry):
        for k in range(TOP_K):
            pltpu.make_async_copy(
                y_ref.at[pl.ds(pl.multiple_of(dest_ref[0, 0, k * tc + t] * ROW_TILES, ROW_TILES), ROW_TILES)],
                buf_ref.at[k, pl.ds(pl.multiple_of(t * ROW_TILES, ROW_TILES), ROW_TILES)], sem).start()
        return carry

    lax.fori_loop(0, tc, issue, 0)

    def drain(t, carry):
        for k in range(TOP_K):
            pltpu.make_async_copy(y_ref.at[pl.ds(0, ROW_TILES)], buf_ref.at[k, pl.ds(0, ROW_TILES)], sem).wait()
        return carry

    lax.fori_loop(0, tc, drain, 0)

    gates = gate_ref[...]
    gates_t = jnp.concatenate([gates, jnp.zeros((SUBLANES - TOP_K, tc), F32)], axis=0).T
    moe = None
    for k in range(TOP_K):
        part = gates_t[:, k:k + 1] * _load_row_tiled(buf_ref, tc, lead=k)
        moe = part if moe is None else moe + part
    xo = x1_ref[...] + moe
    ms = jnp.mean(xo * xo, axis=-1, keepdims=True)
    o_ref[...] = xo * lax.rsqrt(ms + RMS_EPS) * fw_ref[...]


def _combine(x1, gates, dest, y, final_w, tc):
    n = x1.shape[0]
    tiles = n // tc
    dest_t = dest.reshape(TOP_K, tiles, tc).transpose(1, 0, 2).reshape(tiles, 1, TOP_K * tc)
    fw = final_w.reshape(1, D_MODEL)
    return pl.pallas_call(
        functools.partial(_combine_kernel, tc=tc),
        grid=(tiles,),
        in_specs=[pl.BlockSpec((1, 1, TOP_K * tc), lambda i: (i, 0, 0), memory_space=pltpu.SMEM),
                  pl.BlockSpec((tc, D_MODEL), lambda i: (i, 0)),
                  pl.BlockSpec((TOP_K, tc), lambda i: (0, i)),
                  pl.BlockSpec((1, D_MODEL), lambda i: (0, 0)),
                  pl.BlockSpec(memory_space=pl.ANY)],
        out_specs=pl.BlockSpec((tc, D_MODEL), lambda i: (i, 0)),
        out_shape=jax.ShapeDtypeStruct((n, D_MODEL), F32),
        scratch_shapes=[pltpu.VMEM((TOP_K, tc * ROW_TILES, LANES), F32), pltpu.SemaphoreType.DMA(())],
        compiler_params=_params(("arbitrary",)),
        name="combine",
    )(dest_t, x1, gates, fw, y)


PROJ_ROWS = 512
EXPERT_ROWS = 256
DISPATCH_ROWS = 512
COMBINE_ROWS = 256


def _slot_layout(counts, n_pairs, mb):
    padded = (counts + mb - 1) // mb * mb
    pad_end = jnp.cumsum(padded)
    pad_start = pad_end - padded
    n_blocks = n_pairs // mb + N_EXPERTS
    block_row0 = jnp.arange(n_blocks, dtype=jnp.int32) * mb
    block_expert = jnp.minimum(
        jnp.sum((pad_end[None, :] <= block_row0[:, None]).astype(jnp.int32), axis=1), N_EXPERTS - 1)
    n_used = (pad_end[-1:] // mb).astype(jnp.int32)
    return pad_start.astype(jnp.int32), (pad_start + counts).astype(jnp.int32), (padded - counts).astype(jnp.int32), \
        block_expert, n_used, n_blocks


def _layer(x2, batch, seq, norm1_w, w_in, ret_norm_w, w_out, norm2_w, w_router, b_router, w1, b1, w2, b2,
           final_w):
    n = x2.shape[0]
    tm = min(PROJ_ROWS, seq)
    rq, rk, rv, rg, mq, mk, mvt = _in_projection(x2, norm1_w, w_in, batch, seq, tm)
    ret = _retention(rq, rk, rv, rg, ret_norm_w, batch, seq)
    moba = _moba(mq, mk, mvt, batch, seq)
    x1, h2_tiled, top_e, gates, rank, cnt = _route(x2, ret, moba, w_out, norm2_w, w_router, b_router, tm)
    counts = cnt[:, 0].astype(jnp.int32)
    mb = EXPERT_ROWS
    pad_start, fill_lo, fill_n, block_expert, n_used, n_blocks = _slot_layout(counts, n * TOP_K, mb)
    expert_ids = jnp.arange(N_EXPERTS, dtype=jnp.int32)[:, None, None]
    dest = rank + jnp.sum(jnp.where(top_e[None] == expert_ids, pad_start[:, None, None], 0), axis=0)
    xs = _dispatch(h2_tiled, dest, fill_lo, fill_n, n_blocks * mb, min(DISPATCH_ROWS, n))
    y = _experts(xs, block_expert, n_used, w1, b1, w2, b2, mb)
    return _combine(x1, gates, dest, y, final_w, min(COMBINE_ROWS, n))


def kernel(x, norm1_w, w_in, ret_norm_w, w_out, norm2_w, w_router, b_router, w1, b1, w2, b2, final_norm_w):
    batch, seq, d = x.shape
    depth = w_in.shape[0]
    assert depth == 1 and d == D_MODEL and seq % MOBA_BLOCK == 0
    x2 = x.reshape(batch * seq, d)
    out = _layer(x2, batch, seq, norm1_w[0], w_in[0], ret_norm_w[0], w_out[0], norm2_w[0], w_router[0],
                 b_router[0], w1[0], b1[0], w2[0], b2[0], final_norm_w)
    return out.reshape(batch, seq, d)
```

```python
import functools

import numpy as np
import jax
import jax.numpy as jnp
from jax import lax
from jax.experimental import pallas as pl
from jax.experimental.pallas import tpu as pltpu

D_MODEL = 1024
RET_HEADS = 4
RET_DK = 64
RET_DV = 128
RET_CHUNK = 128
RET_THETA = 10000.0
MOBA_HEADS = 8
MOBA_DH = 64
MOBA_BLOCK = 256
MOBA_TOPK = 3
ROPE_THETA = 500000.0
ROPE_DIM = MOBA_DH // 4
RET_QK_W = RET_HEADS * RET_DK
RET_V_W = RET_HEADS * RET_DV
MOBA_W = MOBA_HEADS * MOBA_DH
N_EXPERTS = 32
TOP_K = 4
D_EXPERT = D_MODEL
SWIGLU_LIMIT = 7.0
SWIGLU_ALPHA = 1.702
RMS_EPS = 1e-5
GN_EPS = 1e-6
NEG_INF = -1e30

SUBLANES = 8
LANES = 128
ROW_TILES = D_MODEL // LANES
VMEM_LIMIT = 48 * 1024 * 1024

F32 = jnp.float32
BF16 = jnp.bfloat16
NT_DIMS = (((1,), (1,)), ((), ()))
TN_DIMS = (((0,), (0,)), ((), ()))


def _params(sem, vmem=VMEM_LIMIT):
    return pltpu.CompilerParams(dimension_semantics=sem, vmem_limit_bytes=vmem)


def _rotate(x, cos, sin_lo, sin_hi, half):
    width = x.shape[-1]
    nxt = pltpu.roll(x, width - half, axis=1)
    prv = pltpu.roll(x, half, axis=1)
    return x * cos + nxt * sin_lo + prv * sin_hi


def _inproj_kernel(x_ref, nw_ref, w_ref, wvt_ref, rc_ref, rlo_ref, rhi_ref, mc_ref, mlo_ref, mhi_ref,
                   rq_ref, rk_ref, rv_ref, rg_ref, mq_ref, mk_ref, mvt_ref):
    x = x_ref[...]
    ms = jnp.mean(x * x, axis=-1, keepdims=True)
    h = (x * lax.rsqrt(ms + RMS_EPS) * nw_ref[...]).astype(BF16)
    proj = jnp.dot(h, w_ref[...], preferred_element_type=F32)
    o = 0
    rq = proj[:, o:o + RET_QK_W]; o += RET_QK_W
    rk = proj[:, o:o + RET_QK_W]; o += RET_QK_W
    rv = proj[:, o:o + RET_V_W]; o += RET_V_W
    rg = proj[:, o:o + RET_V_W]; o += RET_V_W
    mq = proj[:, o:o + MOBA_W]; o += MOBA_W
    mk = proj[:, o:o + MOBA_W]
    rc, rlo, rhi = rc_ref[...], rlo_ref[...], rhi_ref[...]
    mc, mlo, mhi = mc_ref[...], mlo_ref[...], mhi_ref[...]
    rq_ref[...] = _rotate(rq, rc, rlo, rhi, RET_DK // 2).astype(BF16)
    rk_ref[...] = _rotate(rk, rc, rlo, rhi, RET_DK // 2).astype(BF16)
    rv_ref[...] = rv.astype(BF16)
    rg_ref[...] = rg.astype(BF16)
    mq_ref[...] = (_rotate(mq, mc, mlo, mhi, ROPE_DIM // 2) * MOBA_DH ** -0.5).astype(BF16)
    mk_ref[...] = _rotate(mk, mc, mlo, mhi, ROPE_DIM // 2).astype(BF16)
    mvt_ref[0] = lax.dot_general(wvt_ref[...], h, NT_DIMS, preferred_element_type=F32).astype(BF16)


def _rotary_tables(seq, rot_dim, theta, head_dim, heads):
    half = rot_dim // 2
    inv_freq = (np.float32(theta) ** (np.float32(-2.0) * np.arange(half, dtype=np.float32) / np.float32(rot_dim)))
    ang = np.arange(seq, dtype=np.float32)[:, None] * inv_freq.astype(np.float32)[None, :]
    cos, sin = np.cos(ang).astype(np.float32), np.sin(ang).astype(np.float32)
    rest = head_dim - rot_dim
    ones = np.ones((seq, rest), np.float32)
    zeros = np.zeros((seq, rest), np.float32)
    zh = np.zeros((seq, half), np.float32)
    cos_t = np.concatenate([cos, cos, ones], axis=-1)
    lo_t = np.concatenate([-sin, zh, zeros], axis=-1)
    hi_t = np.concatenate([zh, sin, zeros], axis=-1)
    tile = lambda t: jnp.asarray(np.tile(t, (1, heads)))
    return tile(cos_t), tile(lo_t), tile(hi_t)


def _in_projection(x2, norm_w, w_in, batch, seq, tm):
    n = x2.shape[0]
    st = seq // tm
    w_main = w_in[:, :D_MODEL * 3 - MOBA_W].astype(BF16)
    w_vt = w_in[:, D_MODEL * 3 - MOBA_W:].T.astype(BF16)
    wm = w_main.shape[1]
    r_tabs = _rotary_tables(seq, RET_DK, RET_THETA, RET_DK, RET_HEADS)
    m_tabs = _rotary_tables(seq, ROPE_DIM, ROPE_THETA, MOBA_DH, MOBA_HEADS)
    row = lambda w: pl.BlockSpec((tm, w), lambda b, s: (b * st + s, 0))
    pos = lambda w: pl.BlockSpec((tm, w), lambda b, s: (s, 0))
    full = lambda a: pl.BlockSpec(a.shape, lambda b, s: (0,) * a.ndim)
    nw = norm_w.reshape(1, D_MODEL)
    out_shape = (
        jax.ShapeDtypeStruct((n, RET_QK_W), BF16), jax.ShapeDtypeStruct((n, RET_QK_W), BF16),
        jax.ShapeDtypeStruct((n, RET_V_W), BF16), jax.ShapeDtypeStruct((n, RET_V_W), BF16),
        jax.ShapeDtypeStruct((n, MOBA_W), BF16), jax.ShapeDtypeStruct((n, MOBA_W), BF16),
        jax.ShapeDtypeStruct((batch, MOBA_W, seq), BF16),
    )
    return pl.pallas_call(
        _inproj_kernel,
        grid=(batch, st),
        in_specs=[row(D_MODEL), full(nw), full(w_main), full(w_vt),
                  pos(RET_QK_W), pos(RET_QK_W), pos(RET_QK_W), pos(MOBA_W), pos(MOBA_W), pos(MOBA_W)],
        out_specs=(row(RET_QK_W), row(RET_QK_W), row(RET_V_W), row(RET_V_W), row(MOBA_W), row(MOBA_W),
                   pl.BlockSpec((1, MOBA_W, tm), lambda b, s: (b, 0, s))),
        out_shape=out_shape,
        compiler_params=_params(("parallel", "parallel")),
        name="in_projection",
    )(x2, nw, w_main, w_vt, *r_tabs, *m_tabs)


def _retention_kernel(q_ref, k_ref, v_ref, g_ref, decay_ref, xi_ref, zeta_ref, gnw_ref, o_ref, state_ref,
                      *, n_chunks, chunk_decay):
    c = RET_CHUNK
    state_ref[...] = jnp.zeros_like(state_ref)

    def chunk(ci, carry):
        r0 = pl.multiple_of(ci * c, c)
        q = q_ref[pl.ds(r0, c), :]
        k = k_ref[pl.ds(r0, c), :]
        v = v_ref[pl.ds(r0, c), :]
        g = g_ref[pl.ds(r0, c), :].astype(F32)
        qx = (q.astype(F32) * xi_ref[...]).astype(BF16)
        kz = (k.astype(F32) * zeta_ref[...]).astype(BF16)
        outs = []
        for h in range(RET_HEADS):
            qs = slice(h * RET_DK, (h + 1) * RET_DK)
            vs = slice(h * RET_DV, (h + 1) * RET_DV)
            vh = v[:, vs]
            scores = lax.dot_general(q[:, qs], k[:, qs], NT_DIMS, preferred_element_type=F32) * decay_ref[h]
            inner = jnp.dot(scores.astype(BF16), vh, preferred_element_type=F32)
            state = state_ref[h]
            cross = jnp.dot(qx[:, qs], state.astype(BF16), preferred_element_type=F32)
            state_ref[h] = chunk_decay[h] * state + lax.dot_general(
                kz[:, qs], vh, TN_DIMS, preferred_element_type=F32)
            ret = inner + cross
            mu = jnp.mean(ret, axis=-1, keepdims=True)
            cen = ret - mu
            var = jnp.mean(cen * cen, axis=-1, keepdims=True)
            outs.append(cen * lax.rsqrt(var + GN_EPS))
        normed = jnp.concatenate(outs, axis=-1) * gnw_ref[...]
        o_ref[pl.ds(r0, c), :] = (g * jax.nn.sigmoid(g) * normed).astype(BF16)
        return carry

    lax.fori_loop(0, n_chunks, chunk, 0)


def _retention_consts():
    h = np.arange(RET_HEADS, dtype=np.float32)
    log_g = np.log1p(-np.exp2(-5.0 - h)).astype(np.float32)
    idx = np.arange(RET_CHUNK, dtype=np.float32)
    rel = idx[:, None] - idx[None, :]
    scale = np.float32(RET_DK ** -0.5)
    decay = np.where(rel >= 0, np.exp(log_g[:, None, None] * np.maximum(rel, 0.0)), 0.0).astype(np.float32) * scale
    zeta = np.exp(log_g[:, None] * (RET_CHUNK - 1.0 - idx)).astype(np.float32) * scale
    xi = np.exp(log_g[:, None] * (idx + 1.0)).astype(np.float32)
    spread = lambda t: np.repeat(t.T, RET_DK, axis=1)
    chunk_decay = tuple(float(v) for v in np.exp(log_g * RET_CHUNK).astype(np.float32))
    return decay, spread(xi), spread(zeta), chunk_decay


def _retention(rq, rk, rv, rg, gn_w, batch, seq):
    n = rq.shape[0]
    decay, xi, zeta, chunk_decay = _retention_consts()
    gnw = gn_w.reshape(1, RET_V_W).astype(F32)
    seq_blk = lambda w: pl.BlockSpec((seq, w), lambda b: (b, 0))
    full = lambda a: pl.BlockSpec(a.shape, lambda b: (0,) * a.ndim)
    kern = functools.partial(_retention_kernel, n_chunks=seq // RET_CHUNK, chunk_decay=chunk_decay)
    return pl.pallas_call(
        kern,
        grid=(batch,),
        in_specs=[seq_blk(RET_QK_W), seq_blk(RET_QK_W), seq_blk(RET_V_W), seq_blk(RET_V_W),
                  full(decay), full(xi), full(zeta), full(gnw)],
        out_specs=seq_blk(RET_V_W),
        out_shape=jax.ShapeDtypeStruct((n, RET_V_W), BF16),
        scratch_shapes=[pltpu.VMEM((RET_HEADS, RET_DK, RET_DV), F32)],
        compiler_params=_params(("parallel",)),
        name="retention",
    )(rq, rk, rv, rg, jnp.asarray(decay), jnp.asarray(xi), jnp.asarray(zeta), gnw)


HEADS_PER_STEP = LANES // MOBA_DH


def _moba_kernel(q_ref, k_ref, vt_ref, o_ref, *, n_blocks):
    blk = MOBA_BLOCK
    dh = MOBA_DH
    blk_id = lax.broadcasted_iota(jnp.int32, (n_blocks, blk), 0)
    key_pos = lax.broadcasted_iota(jnp.int32, (blk, blk), 0)
    qry_pos = lax.broadcasted_iota(jnp.int32, (blk, blk), 1)
    causal = key_pos <= qry_pos
    rows = lambda b: slice(b * blk, (b + 1) * blk)

    def head_block(hh, j, k_mean):
        lanes = slice(hh * dh, (hh + 1) * dh)
        q = q_ref[rows(j), lanes]
        n_keys = (j + 1) * blk
        s_all = lax.dot_general(k_ref[0:n_keys, lanes], q, NT_DIMS, preferred_element_type=F32)
        if j > 0:
            gate = lax.dot_general(k_mean, q.astype(F32), NT_DIMS, preferred_element_type=F32)
            past = blk_id < j
            gate = jnp.where(past, gate, NEG_INF)
            rank = jnp.zeros((n_blocks, blk), jnp.int32)
            for other in range(j):
                g_o = gate[other:other + 1, :]
                ahead = (g_o > gate) | ((g_o == gate) & (other < blk_id))
                rank = rank + ahead.astype(jnp.int32)
            bias = jnp.where(past & (rank < MOBA_TOPK), 0.0, NEG_INF)
        s_blocks = [s_all[rows(i), :] + bias[i:i + 1, :] for i in range(j)]
        s_blocks.append(jnp.where(causal, s_all[rows(j), :], NEG_INF))
        m = jnp.max(s_blocks[0], axis=0, keepdims=True)
        for s in s_blocks[1:]:
            m = jnp.maximum(m, jnp.max(s, axis=0, keepdims=True))
        p_blocks = [jnp.exp(s - m) for s in s_blocks]
        l = jnp.sum(p_blocks[0], axis=0, keepdims=True)
        for p in p_blocks[1:]:
            l = l + jnp.sum(p, axis=0, keepdims=True)
        p_all = jnp.concatenate([p.astype(BF16) for p in p_blocks], axis=0)
        acc = jnp.dot(vt_ref[0, lanes, 0:n_keys], p_all, preferred_element_type=F32)
        return (acc / l).T

    k_means = []
    for hh in range(HEADS_PER_STEP):
        kf = k_ref[:, hh * dh:(hh + 1) * dh].astype(F32)
        k_means.append(jnp.mean(kf.reshape(n_blocks, blk, dh), axis=1))
    for j in range(n_blocks):
        outs = [head_block(hh, j, k_means[hh]) for hh in range(HEADS_PER_STEP)]
        o_ref[rows(j), :] = jnp.concatenate(outs, axis=-1).astype(BF16)


def _moba(mq, mk, mvt, batch, seq):
    n = mq.shape[0]
    n_blocks = seq // MOBA_BLOCK
    pairs = MOBA_HEADS // HEADS_PER_STEP
    qk_spec = pl.BlockSpec((seq, LANES), lambda b, p: (b, p))
    kern = functools.partial(_moba_kernel, n_blocks=n_blocks)
    return pl.pallas_call(
        kern,
        grid=(batch, pairs),
        in_specs=[qk_spec, qk_spec, pl.BlockSpec((1, LANES, seq), lambda b, p: (b, p, 0))],
        out_specs=qk_spec,
        out_shape=jax.ShapeDtypeStruct((n, MOBA_W), BF16),
        compiler_params=_params(("parallel", "parallel")),
        name="moba",
    )(mq, mk, mvt)


def _store_row_tiled(ref, val):
    rows = val.shape[0]
    for s in range(ROW_TILES):
        ref[pl.ds(s, rows, stride=ROW_TILES), :] = val[:, s * LANES:(s + 1) * LANES]


def _load_row_tiled(ref, rows, lead=None):
    parts = []
    for s in range(ROW_TILES):
        idx = (pl.ds(s, rows, stride=ROW_TILES), slice(None))
        parts.append(ref[idx] if lead is None else ref[(lead,) + idx])
    return jnp.concatenate(parts, axis=-1)


def _route_kernel(x_ref, ret_ref, moba_ref, wor_ref, wom_ref, nw_ref, wrt_ref, br_ref, tri_ref,
                  x1_ref, h2_ref, te_ref, gate_ref, rank_ref, cnt_ref, carry_ref):
    @pl.when(pl.program_id(0) == 0)
    def _():
        carry_ref[...] = jnp.zeros_like(carry_ref)

    x1 = (x_ref[...]
          + jnp.dot(ret_ref[...], wor_ref[...], preferred_element_type=F32)
          + jnp.dot(moba_ref[...], wom_ref[...], preferred_element_type=F32))
    x1_ref[...] = x1
    ms = jnp.mean(x1 * x1, axis=-1, keepdims=True)
    h2 = x1 * lax.rsqrt(ms + RMS_EPS) * nw_ref[...]
    _store_row_tiled(h2_ref, h2)
    logits = lax.dot_general(wrt_ref[...], h2, NT_DIMS, precision=lax.Precision.HIGHEST,
                             preferred_element_type=F32) + br_ref[...]
    tm = logits.shape[1]
    e_id = lax.broadcasted_iota(jnp.int32, (N_EXPERTS, tm), 0)
    vals, ids = [], []
    for _ in range(TOP_K):
        m = jnp.max(logits, axis=0, keepdims=True)
        idx = jnp.min(jnp.where(logits == m, e_id, N_EXPERTS), axis=0, keepdims=True)
        vals.append(m)
        ids.append(idx)
        logits = jnp.where(e_id == idx, -jnp.inf, logits)
    exps = [jnp.exp(v - vals[0]) for v in vals]
    denom = exps[0] + exps[1] + exps[2] + exps[3]
    gate_ref[...] = jnp.concatenate([e / denom for e in exps], axis=0)
    te_ref[...] = jnp.concatenate(ids, axis=0)
    hits = [e_id == idx for idx in ids]
    picked = (hits[0] | hits[1] | hits[2] | hits[3])
    picked_f = jnp.where(picked, 1.0, 0.0)
    before = jnp.dot(picked_f.astype(BF16), tri_ref[...], preferred_element_type=F32) + carry_ref[...]
    ranks = [jnp.sum(jnp.where(hit, before, 0.0), axis=0, keepdims=True) for hit in hits]
    rank_ref[...] = jnp.concatenate(ranks, axis=0).astype(jnp.int32)
    carry_ref[...] = carry_ref[...] + jnp.sum(picked_f, axis=1, keepdims=True)
    cnt_ref[...] = jnp.broadcast_to(carry_ref[...], cnt_ref.shape)


def _route(x2, ret, moba, w_out, norm_w, w_router, b_router, tm):
    n = x2.shape[0]
    wo_r = w_out[:RET_V_W].astype(BF16)
    wo_m = w_out[RET_V_W:].astype(BF16)
    nw = norm_w.reshape(1, D_MODEL)
    wrt = w_router.T.astype(F32)
    br = b_router.reshape(N_EXPERTS, 1).astype(F32)
    tri = (np.arange(tm)[:, None] < np.arange(tm)[None, :]).astype(np.float32)
    tri = jnp.asarray(tri, dtype=BF16)
    row = lambda w: pl.BlockSpec((tm, w), lambda i: (i, 0))
    col = pl.BlockSpec((TOP_K, tm), lambda i: (0, i))
    full = lambda a: pl.BlockSpec(a.shape, lambda i: (0,) * a.ndim)
    out_shape = (
        jax.ShapeDtypeStruct((n, D_MODEL), F32),
        jax.ShapeDtypeStruct((n * ROW_TILES, LANES), F32),
        jax.ShapeDtypeStruct((TOP_K, n), jnp.int32),
        jax.ShapeDtypeStruct((TOP_K, n), F32),
        jax.ShapeDtypeStruct((TOP_K, n), jnp.int32),
        jax.ShapeDtypeStruct((N_EXPERTS, LANES), F32),
    )
    return pl.pallas_call(
        _route_kernel,
        grid=(n // tm,),
        in_specs=[row(D_MODEL), row(RET_V_W), row(MOBA_W), full(wo_r), full(wo_m), full(nw), full(wrt),
                  full(br), full(tri)],
        out_specs=(row(D_MODEL), pl.BlockSpec((tm * ROW_TILES, LANES), lambda i: (i, 0)), col, col, col,
                   pl.BlockSpec((N_EXPERTS, LANES), lambda i: (0, 0))),
        out_shape=out_shape,
        scratch_shapes=[pltpu.VMEM((N_EXPERTS, 1), F32)],
        compiler_params=_params(("arbitrary",)),
        name="out_proj_route",
    )(x2, ret, moba, wo_r, wo_m, nw, wrt, br, tri)


def _row_slice(ref, row):
    return ref.at[pl.ds(pl.multiple_of(row * ROW_TILES, ROW_TILES), ROW_TILES)]


DISPATCH_UNROLL = 4


def _dispatch_kernel(fill_lo_ref, fill_n_ref, dest_ref, h2_ref, xs_ref, zero_ref, sem, fill_sem, *, tt):
    @pl.when(pl.program_id(0) == 0)
    def _():
        zero_ref[...] = jnp.zeros_like(zero_ref)

        def fill_copy(e, r):
            return pltpu.make_async_copy(zero_ref, _row_slice(xs_ref, fill_lo_ref[e] + r), fill_sem)

        def fill(e, carry):
            def one(r, c):
                fill_copy(e, r).start()
                return c
            return lax.fori_loop(0, fill_n_ref[e], one, carry)

        def fill_wait(e, carry):
            def one(r, c):
                fill_copy(e, r).wait()
                return c
            return lax.fori_loop(0, fill_n_ref[e], one, carry)

        lax.fori_loop(0, N_EXPERTS, fill, 0)
        lax.fori_loop(0, N_EXPERTS, fill_wait, 0)

    def row_copy(t, k):
        return pltpu.make_async_copy(_row_slice(h2_ref, t), _row_slice(xs_ref, dest_ref[0, 0, k * tt + t]), sem)

    def issue(g, carry):
        for u in range(DISPATCH_UNROLL):
            for k in range(TOP_K):
                row_copy(g * DISPATCH_UNROLL + u, k).start()
        return carry

    lax.fori_loop(0, tt // DISPATCH_UNROLL, issue, 0)

    def drain(g, carry):
        for u in range(DISPATCH_UNROLL):
            for k in range(TOP_K):
                row_copy(g * DISPATCH_UNROLL + u, k).wait()
        return carry

    lax.fori_loop(0, tt // DISPATCH_UNROLL, drain, 0)


def _dispatch(h2_tiled, dest, fill_lo, fill_n, n_slots, tt):
    n = dest.shape[1]
    tiles = n // tt
    dest_t = dest.reshape(TOP_K, tiles, tt).transpose(1, 0, 2).reshape(tiles, 1, TOP_K * tt)
    grid_spec = pltpu.PrefetchScalarGridSpec(
        num_scalar_prefetch=2,
        grid=(tiles,),
        in_specs=[pl.BlockSpec((1, 1, TOP_K * tt), lambda i, lo, cnt: (i, 0, 0), memory_space=pltpu.SMEM),
                  pl.BlockSpec((tt * ROW_TILES, LANES), lambda i, lo, cnt: (i, 0))],
        out_specs=pl.BlockSpec(memory_space=pl.ANY),
        scratch_shapes=[pltpu.VMEM((ROW_TILES, LANES), F32), pltpu.SemaphoreType.DMA(()),
                        pltpu.SemaphoreType.DMA(())],
    )
    return pl.pallas_call(
        functools.partial(_dispatch_kernel, tt=tt),
        grid_spec=grid_spec,
        out_shape=jax.ShapeDtypeStruct((n_slots * ROW_TILES, LANES), F32),
        compiler_params=_params(("arbitrary",)),
        name="dispatch",
    )(fill_lo, fill_n, dest_t, h2_tiled)


def _expert_kernel(be_ref, used_ref, xs_ref, w1_ref, b1_ref, w2_ref, b2_ref, y_ref, *, mb):
    b = pl.program_id(0)

    @pl.when(b < used_ref[0])
    def _():
        x = _load_row_tiled(xs_ref, mb).astype(BF16)
        u = jnp.dot(x, w1_ref[0], preferred_element_type=F32) + b1_ref[0]
        gate = jnp.minimum(u[:, :D_EXPERT], SWIGLU_LIMIT)
        lin = jnp.clip(u[:, D_EXPERT:], -SWIGLU_LIMIT, SWIGLU_LIMIT)
        act = gate * jax.nn.sigmoid(SWIGLU_ALPHA * gate) * (lin + 1.0)
        y = jnp.dot(act.astype(BF16), w2_ref[0], preferred_element_type=F32) + b2_ref[0]
        _store_row_tiled(y_ref, y)

    @pl.when(b >= used_ref[0])
    def _():
        y_ref[...] = jnp.zeros_like(y_ref)


def _experts(xs, block_expert, n_used, w1, b1, w2, b2, mb):
    n_blocks = block_expert.shape[0]
    w1b = w1.astype(BF16)
    w2b = w2.astype(BF16)
    b1r = b1.reshape(N_EXPERTS, 1, 2 * D_EXPERT).astype(F32)
    b2r = b2.reshape(N_EXPERTS, 1, D_MODEL).astype(F32)
    grid_spec = pltpu.PrefetchScalarGridSpec(
        num_scalar_prefetch=2,
        grid=(n_blocks,),
        in_specs=[
            pl.BlockSpec((mb * ROW_TILES, LANES), lambda i, be, used: (jnp.minimum(i, used[0] - 1), 0)),
            pl.BlockSpec((1, D_MODEL, 2 * D_EXPERT), lambda i, be, used: (be[i], 0, 0)),
            pl.BlockSpec((1, 1, 2 * D_EXPERT), lambda i, be, used: (be[i], 0, 0)),
            pl.BlockSpec((1, D_EXPERT, D_MODEL), lambda i, be, used: (be[i], 0, 0)),
            pl.BlockSpec((1, 1, D_MODEL), lambda i, be, used: (be[i], 0, 0)),
        ],
        out_specs=pl.BlockSpec((mb * ROW_TILES, LANES), lambda i, be, used: (i, 0)),
    )
    return pl.pallas_call(
        functools.partial(_expert_kernel, mb=mb),
        grid_spec=grid_spec,
        out_shape=jax.ShapeDtypeStruct(xs.shape, F32),
        compiler_params=_params(("arbitrary",)),
        name="experts",
    )(block_expert, n_used, xs, w1b, b1r, w2b, b2r)


def _combine_kernel(dest_ref, x1_ref, gate_ref, fw_ref, y_ref, o_ref, buf_ref, sem, *, tc):
    def issue(t, carry):
        for k in range(TOP_K):
            pltpu.make_async_copy(
                y_ref.at[pl.ds(pl.multiple_of(dest_ref[0, 0, k * tc + t] * ROW_TILES, ROW_TILES), ROW_TILES)],
                buf_ref.at[k, pl.ds(pl.multiple_of(t * ROW_TILES, ROW_TILES), ROW_TILES)], sem).start()
        return carry

    lax.fori_loop(0, tc, issue, 0)

    def drain(t, carry):
        for k in range(TOP_K):
            pltpu.make_async_copy(y_ref.at[pl.ds(0, ROW_TILES)], buf_ref.at[k, pl.ds(0, ROW_TILES)], sem).wait()
        return carry

    lax.fori_loop(0, tc, drain, 0)

    gates = gate_ref[...]
    gates_t = jnp.concatenate([gates, jnp.zeros((SUBLANES - TOP_K, tc), F32)], axis=0).T
    moe = None
    for k in range(TOP_K):
        part = gates_t[:, k:k + 1] * _load_row_tiled(buf_ref, tc, lead=k)
        moe = part if moe is None else moe + part
    xo = x1_ref[...] + moe
    ms = jnp.mean(xo * xo, axis=-1, keepdims=True)
    o_ref[...] = xo * lax.rsqrt(ms + RMS_EPS) * fw_ref[...]


def _combine(x1, gates, dest, y, final_w, tc):
    n = x1.shape[0]
    tiles = n // tc
    dest_t = dest.reshape(TOP_K, tiles, tc).transpose(1, 0, 2).reshape(tiles, 1, TOP_K * tc)
    fw = final_w.reshape(1, D_MODEL)
    return pl.pallas_call(
        functools.partial(_combine_kernel, tc=tc),
        grid=(tiles,),
        in_specs=[pl.BlockSpec((1, 1, TOP_K * tc), lambda i: (i, 0, 0), memory_space=pltpu.SMEM),
                  pl.BlockSpec((tc, D_MODEL), lambda i: (i, 0)),
                  pl.BlockSpec((TOP_K, tc), lambda i: (0, i)),
                  pl.BlockSpec((1, D_MODEL), lambda i: (0, 0)),
                  pl.BlockSpec(memory_space=pl.ANY)],
        out_specs=pl.BlockSpec((tc, D_MODEL), lambda i: (i, 0)),
        out_shape=jax.ShapeDtypeStruct((n, D_MODEL), F32),
        scratch_shapes=[pltpu.VMEM((TOP_K, tc * ROW_TILES, LANES), F32), pltpu.SemaphoreType.DMA(())],
        compiler_params=_params(("arbitrary",)),
        name="combine",
    )(dest_t, x1, gates, fw, y)


PROJ_ROWS = 512
EXPERT_ROWS = 256
DISPATCH_ROWS = 512
COMBINE_ROWS = 256


def _slot_layout(counts, n_pairs, mb):
    padded = (counts + mb - 1) // mb * mb
    pad_end = jnp.cumsum(padded)
    pad_start = pad_end - padded
    n_blocks = n_pairs // mb + N_EXPERTS
    block_row0 = jnp.arange(n_blocks, dtype=jnp.int32) * mb
    block_expert = jnp.minimum(
        jnp.sum((pad_end[None, :] <= block_row0[:, None]).astype(jnp.int32), axis=1), N_EXPERTS - 1)
    n_used = (pad_end[-1:] // mb).astype(jnp.int32)
    return pad_start.astype(jnp.int32), (pad_start + counts).astype(jnp.int32), (padded - counts).astype(jnp.int32), \
        block_expert, n_used, n_blocks


def _layer(x2, batch, seq, norm1_w, w_in, ret_norm_w, w_out, norm2_w, w_router, b_router, w1, b1, w2, b2,
           final_w):
    n = x2.shape[0]
    tm = min(PROJ_ROWS, seq)
    rq, rk, rv, rg, mq, mk, mvt = _in_projection(x2, norm1_w, w_in, batch, seq, tm)
    ret = _retention(rq, rk, rv, rg, ret_norm_w, batch, seq)
    moba = _moba(mq, mk, mvt, batch, seq)
    x1, h2_tiled, top_e, gates, rank, cnt = _route(x2, ret, moba, w_out, norm2_w, w_router, b_router, tm)
    counts = cnt[:, 0].astype(jnp.int32)
    mb = EXPERT_ROWS
    pad_start, fill_lo, fill_n, block_expert, n_used, n_blocks = _slot_layout(counts, n * TOP_K, mb)
    expert_ids = jnp.arange(N_EXPERTS, dtype=jnp.int32)[:, None, None]
    dest = rank + jnp.sum(jnp.where(top_e[None] == expert_ids, pad_start[:, None, None], 0), axis=0)
    xs = _dispatch(h2_tiled, dest, fill_lo, fill_n, n_blocks * mb, min(DISPATCH_ROWS, n))
    y = _experts(xs, block_expert, n_used, w1, b1, w2, b2, mb)
    return _combine(x1, gates, dest, y, final_w, min(COMBINE_ROWS, n))


def kernel(x, norm1_w, w_in, ret_norm_w, w_out, norm2_w, w_router, b_router, w1, b1, w2, b2, final_norm_w):
    batch, seq, d = x.shape
    depth = w_in.shape[0]
    assert depth == 1 and d == D_MODEL and seq % MOBA_BLOCK == 0
    x2 = x.reshape(batch * seq, d)
    out = _layer(x2, batch, seq, norm1_w[0], w_in[0], ret_norm_w[0], w_out[0], norm2_w[0], w_router[0],
                 b_router[0], w1[0], b1[0], w2[0], b2[0], final_norm_w)
    return out.reshape(batch, seq, d)
```

```python
import functools

import numpy as np
import jax
import jax.numpy as jnp
from jax import lax
from jax.experimental import pallas as pl
from jax.experimental.pallas import tpu as pltpu

D_MODEL = 1024
RET_HEADS = 4
RET_DK = 64
RET_DV = 128
RET_CHUNK = 128
RET_THETA = 10000.0
MOBA_HEADS = 8
MOBA_DH = 64
MOBA_BLOCK = 256
MOBA_TOPK = 3
ROPE_THETA = 500000.0
ROPE_DIM = MOBA_DH // 4
RET_QK_W = RET_HEADS * RET_DK
RET_V_W = RET_HEADS * RET_DV
MOBA_W = MOBA_HEADS * MOBA_DH
N_EXPERTS = 32
TOP_K = 4
D_EXPERT = D_MODEL
SWIGLU_LIMIT = 7.0
SWIGLU_ALPHA = 1.702
RMS_EPS = 1e-5
GN_EPS = 1e-6
NEG_INF = -1e30

SUBLANES = 8
LANES = 128
ROW_TILES = D_MODEL // LANES
VMEM_LIMIT = 48 * 1024 * 1024

F32 = jnp.float32
BF16 = jnp.bfloat16
NT_DIMS = (((1,), (1,)), ((), ()))
TN_DIMS = (((0,), (0,)), ((), ()))


def _params(sem, vmem=VMEM_LIMIT):
    return pltpu.CompilerParams(dimension_semantics=sem, vmem_limit_bytes=vmem)


def _rotate(x, cos, sin_lo, sin_hi, half):
    width = x.shape[-1]
    nxt = pltpu.roll(x, width - half, axis=1)
    prv = pltpu.roll(x, half, axis=1)
    return x * cos + nxt * sin_lo + prv * sin_hi


def _inproj_kernel(x_ref, nw_ref, w_ref, wvt_ref, rc_ref, rlo_ref, rhi_ref, mc_ref, mlo_ref, mhi_ref,
                   rq_ref, rk_ref, rv_ref, rg_ref, mq_ref, mk_ref, mvt_ref):
    x = x_ref[...]
    ms = jnp.mean(x * x, axis=-1, keepdims=True)
    h = (x * lax.rsqrt(ms + RMS_EPS) * nw_ref[...]).astype(BF16)
    proj = jnp.dot(h, w_ref[...], preferred_element_type=F32)
    o = 0
    rq = proj[:, o:o + RET_QK_W]; o += RET_QK_W
    rk = proj[:, o:o + RET_QK_W]; o += RET_QK_W
    rv = proj[:, o:o + RET_V_W]; o += RET_V_W
    rg = proj[:, o:o + RET_V_W]; o += RET_V_W
    mq = proj[:, o:o + MOBA_W]; o += MOBA_W
    mk = proj[:, o:o + MOBA_W]
    rc, rlo, rhi = rc_ref[...], rlo_ref[...], rhi_ref[...]
    mc, mlo, mhi = mc_ref[...], mlo_ref[...], mhi_ref[...]
    rq_ref[...] = _rotate(rq, rc, rlo, rhi, RET_DK // 2).astype(BF16)
    rk_ref[...] = _rotate(rk, rc, rlo, rhi, RET_DK // 2).astype(BF16)
    rv_ref[...] = rv.astype(BF16)
    rg_ref[...] = rg.astype(BF16)
    mq_ref[...] = (_rotate(mq, mc, mlo, mhi, ROPE_DIM // 2) * MOBA_DH ** -0.5).astype(BF16)
    mk_ref[...] = _rotate(mk, mc, mlo, mhi, ROPE_DIM // 2).astype(BF16)
    mvt_ref[0] = lax.dot_general(wvt_ref[...], h, NT_DIMS, preferred_element_type=F32).astype(BF16)


def _rotary_tables(seq, rot_dim, theta, head_dim, heads):
    half = rot_dim // 2
    inv_freq = (np.float32(theta) ** (np.float32(-2.0) * np.arange(half, dtype=np.float32) / np.float32(rot_dim)))
    ang = np.arange(seq, dtype=np.float32)[:, None] * inv_freq.astype(np.float32)[None, :]
    cos, sin = np.cos(ang).astype(np.float32), np.sin(ang).astype(np.float32)
    rest = head_dim - rot_dim
    ones = np.ones((seq, rest), np.float32)
    zeros = np.zeros((seq, rest), np.float32)
    zh = np.zeros((seq, half), np.float32)
    cos_t = np.concatenate([cos, cos, ones], axis=-1)
    lo_t = np.concatenate([-sin, zh, zeros], axis=-1)
    hi_t = np.concatenate([zh, sin, zeros], axis=-1)
    tile = lambda t: jnp.asarray(np.tile(t, (1, heads)))
    return tile(cos_t), tile(lo_t), tile(hi_t)


def _in_projection(x2, norm_w, w_in, batch, seq, tm):
    n = x2.shape[0]
    st = seq // tm
    w_main = w_in[:, :D_MODEL * 3 - MOBA_W].astype(BF16)
    w_vt = w_in[:, D_MODEL * 3 - MOBA_W:].T.astype(BF16)
    wm = w_main.shape[1]
    r_tabs = _rotary_tables(seq, RET_DK, RET_THETA, RET_DK, RET_HEADS)
    m_tabs = _rotary_tables(seq, ROPE_DIM, ROPE_THETA, MOBA_DH, MOBA_HEADS)
    row = lambda w: pl.BlockSpec((tm, w), lambda b, s: (b * st + s, 0))
    pos = lambda w: pl.BlockSpec((tm, w), lambda b, s: (s, 0))
    full = lambda a: pl.BlockSpec(a.shape, lambda b, s: (0,) * a.ndim)
    nw = norm_w.reshape(1, D_MODEL)
    out_shape = (
        jax.ShapeDtypeStruct((n, RET_QK_W), BF16), jax.ShapeDtypeStruct((n, RET_QK_W), BF16),
        jax.ShapeDtypeStruct((n, RET_V_W), BF16), jax.ShapeDtypeStruct((n, RET_V_W), BF16),
        jax.ShapeDtypeStruct((n, MOBA_W), BF16), jax.ShapeDtypeStruct((n, MOBA_W), BF16),
        jax.ShapeDtypeStruct((batch, MOBA_W, seq), BF16),
    )
    return pl.pallas_call(
        _inproj_kernel,
        grid=(batch, st),
        in_specs=[row(D_MODEL), full(nw), full(w_main), full(w_vt),
                  pos(RET_QK_W), pos(RET_QK_W), pos(RET_QK_W), pos(MOBA_W), pos(MOBA_W), pos(MOBA_W)],
        out_specs=(row(RET_QK_W), row(RET_QK_W), row(RET_V_W), row(RET_V_W), row(MOBA_W), row(MOBA_W),
                   pl.BlockSpec((1, MOBA_W, tm), lambda b, s: (b, 0, s))),
        out_shape=out_shape,
        compiler_params=_params(("parallel", "parallel")),
        name="in_projection",
    )(x2, nw, w_main, w_vt, *r_tabs, *m_tabs)


def _retention_kernel(q_ref, k_ref, v_ref, g_ref, decay_ref, xi_ref, zeta_ref, gnw_ref, o_ref, state_ref,
                      *, n_chunks, chunk_decay):
    c = RET_CHUNK
    state_ref[...] = jnp.zeros_like(state_ref)

    def chunk(ci, carry):
        r0 = pl.multiple_of(ci * c, c)
        q = q_ref[pl.ds(r0, c), :]
        k = k_ref[pl.ds(r0, c), :]
        v = v_ref[pl.ds(r0, c), :]
        g = g_ref[pl.ds(r0, c), :].astype(F32)
        qx = (q.astype(F32) * xi_ref[...]).astype(BF16)
        kz = (k.astype(F32) * zeta_ref[...]).astype(BF16)
        outs = []
        for h in range(RET_HEADS):
            qs = slice(h * RET_DK, (h + 1) * RET_DK)
            vs = slice(h * RET_DV, (h + 1) * RET_DV)
            vh = v[:, vs]
            scores = lax.dot_general(q[:, qs], k[:, qs], NT_DIMS, preferred_element_type=F32) * decay_ref[h]
            inner = jnp.dot(scores.astype(BF16), vh, preferred_element_type=F32)
            state = state_ref[h]
            cross = jnp.dot(qx[:, qs], state.astype(BF16), preferred_element_type=F32)
            state_ref[h] = chunk_decay[h] * state + lax.dot_general(
                kz[:, qs], vh, TN_DIMS, preferred_element_type=F32)
            ret = inner + cross
            mu = jnp.mean(ret, axis=-1, keepdims=True)
            cen = ret - mu
            var = jnp.mean(cen * cen, axis=-1, keepdims=True)
            outs.append(cen * lax.rsqrt(var + GN_EPS))
        normed = jnp.concatenate(outs, axis=-1) * gnw_ref[...]
        o_ref[pl.ds(r0, c), :] = (g * jax.nn.sigmoid(g) * normed).astype(BF16)
        return carry

    lax.fori_loop(0, n_chunks, chunk, 0)


def _retention_consts():
    h = np.arange(RET_HEADS, dtype=np.float32)
    log_g = np.log1p(-np.exp2(-5.0 - h)).astype(np.float32)
    idx = np.arange(RET_CHUNK, dtype=np.float32)
    rel = idx[:, None] - idx[None, :]
    scale = np.float32(RET_DK ** -0.5)
    decay = np.where(rel >= 0, np.exp(log_g[:, None, None] * np.maximum(rel, 0.0)), 0.0).astype(np.float32) * scale
    zeta = np.exp(log_g[:, None] * (RET_CHUNK - 1.0 - idx)).astype(np.float32) * scale
    xi = np.exp(log_g[:, None] * (idx + 1.0)).astype(np.float32)
    spread = lambda t: np.repeat(t.T, RET_DK, axis=1)
    chunk_decay = tuple(float(v) for v in np.exp(log_g * RET_CHUNK).astype(np.float32))
    return decay, spread(xi), spread(zeta), chunk_decay


def _retention(rq, rk, rv, rg, gn_w, batch, seq):
    n = rq.shape[0]
    decay, xi, zeta, chunk_decay = _retention_consts()
    gnw = gn_w.reshape(1, RET_V_W).astype(F32)
    seq_blk = lambda w: pl.BlockSpec((seq, w), lambda b: (b, 0))
    full = lambda a: pl.BlockSpec(a.shape, lambda b: (0,) * a.ndim)
    kern = functools.partial(_retention_kernel, n_chunks=seq // RET_CHUNK, chunk_decay=chunk_decay)
    return pl.pallas_call(
        kern,
        grid=(batch,),
        in_specs=[seq_blk(RET_QK_W), seq_blk(RET_QK_W), seq_blk(RET_V_W), seq_blk(RET_V_W),
                  full(decay), full(xi), full(zeta), full(gnw)],
        out_specs=seq_blk(RET_V_W),
        out_shape=jax.ShapeDtypeStruct((n, RET_V_W), BF16),
        scratch_shapes=[pltpu.VMEM((RET_HEADS, RET_DK, RET_DV), F32)],
        compiler_params=_params(("parallel",)),
        name="retention",
    )(rq, rk, rv, rg, jnp.asarray(decay), jnp.asarray(xi), jnp.asarray(zeta), gnw)


HEADS_PER_STEP = LANES // MOBA_DH


def _moba_kernel(q_ref, k_ref, vt_ref, o_ref, *, n_blocks):
    blk = MOBA_BLOCK
    dh = MOBA_DH
    blk_id = lax.broadcasted_iota(jnp.int32, (n_blocks, blk), 0)
    key_pos = lax.broadcasted_iota(jnp.int32, (blk, blk), 0)
    qry_pos = lax.broadcasted_iota(jnp.int32, (blk, blk), 1)
    causal = key_pos <= qry_pos
    rows = lambda b: slice(b * blk, (b + 1) * blk)

    def head_block(hh, j, k_mean):
        lanes = slice(hh * dh, (hh + 1) * dh)
        q = q_ref[rows(j), lanes]
        n_keys = (j + 1) * blk
        s_all = lax.dot_general(k_ref[0:n_keys, lanes], q, NT_DIMS, preferred_element_type=F32)
        if j > 0:
            gate = lax.dot_general(k_mean, q.astype(F32), NT_DIMS, preferred_element_type=F32)
            past = blk_id < j
            gate = jnp.where(past, gate, NEG_INF)
            rank = jnp.zeros((n_blocks, blk), jnp.int32)
            for other in range(j):
                g_o = gate[other:other + 1, :]
                ahead = (g_o > gate) | ((g_o == gate) & (other < blk_id))
                rank = rank + ahead.astype(jnp.int32)
            bias = jnp.where(past & (rank < MOBA_TOPK), 0.0, NEG_INF)
        s_blocks = [s_all[rows(i), :] + bias[i:i + 1, :] for i in range(j)]
        s_blocks.append(jnp.where(causal, s_all[rows(j), :], NEG_INF))
        m = jnp.max(s_blocks[0], axis=0, keepdims=True)
        for s in s_blocks[1:]:
            m = jnp.maximum(m, jnp.max(s, axis=0, keepdims=True))
        p_blocks = [jnp.exp(s - m) for s in s_blocks]
        l = jnp.sum(p_blocks[0], axis=0, keepdims=True)
        for p in p_blocks[1:]:
            l = l + jnp.sum(p, axis=0, keepdims=True)
        p_all = jnp.concatenate([p.astype(BF16) for p in p_blocks], axis=0)
        acc = jnp.dot(vt_ref[0, lanes, 0:n_keys], p_all, preferred_element_type=F32)
        return (acc / l).T

    k_means = []
    for hh in range(HEADS_PER_STEP):
        kf = k_ref[:, hh * dh:(hh + 1) * dh].astype(F32)
        k_means.append(jnp.mean(kf.reshape(n_blocks, blk, dh), axis=1))
    for j in range(n_blocks):
        outs = [head_block(hh, j, k_means[hh]) for hh in range(HEADS_PER_STEP)]
        o_ref[rows(j), :] = jnp.concatenate(outs, axis=-1).astype(BF16)


def _moba(mq, mk, mvt, batch, seq):
    n = mq.shape[0]
    n_blocks = seq // MOBA_BLOCK
    pairs = MOBA_HEADS // HEADS_PER_STEP
    qk_spec = pl.BlockSpec((seq, LANES), lambda b, p: (b, p))
    kern = functools.partial(_moba_kernel, n_blocks=n_blocks)
    return pl.pallas_call(
        kern,
        grid=(batch, pairs),
        in_specs=[qk_spec, qk_spec, pl.BlockSpec((1, LANES, seq), lambda b, p: (b, p, 0))],
        out_specs=qk_spec,
        out_shape=jax.ShapeDtypeStruct((n, MOBA_W), BF16),
        compiler_params=_params(("parallel", "parallel")),
        name="moba",
    )(mq, mk, mvt)


def _store_row_tiled(ref, val):
    rows = val.shape[0]
    for s in range(ROW_TILES):
        ref[pl.ds(s, rows, stride=ROW_TILES), :] = val[:, s * LANES:(s + 1) * LANES]


def _load_row_tiled(ref, rows, lead=()):
    parts = []
    for s in range(ROW_TILES):
        parts.append(ref[tuple(lead) + (pl.ds(s, rows, stride=ROW_TILES), slice(None))])
    return jnp.concatenate(parts, axis=-1)


def _route_kernel(x_ref, ret_ref, moba_ref, wor_ref, wom_ref, nw_ref, wrt_ref, br_ref, tri_ref,
                  x1_ref, h2_ref, te_ref, gate_ref, rank_ref, cnt_ref, carry_ref):
    @pl.when(pl.program_id(0) == 0)
    def _():
        carry_ref[...] = jnp.zeros_like(carry_ref)

    x1 = (x_ref[...]
          + jnp.dot(ret_ref[...], wor_ref[...], preferred_element_type=F32)
          + jnp.dot(moba_ref[...], wom_ref[...], preferred_element_type=F32))
    x1_ref[...] = x1
    ms = jnp.mean(x1 * x1, axis=-1, keepdims=True)
    h2 = x1 * lax.rsqrt(ms + RMS_EPS) * nw_ref[...]
    _store_row_tiled(h2_ref, h2)
    logits = lax.dot_general(wrt_ref[...], h2, NT_DIMS, precision=lax.Precision.HIGHEST,
                             preferred_element_type=F32) + br_ref[...]
    tm = logits.shape[1]
    e_id = lax.broadcasted_iota(jnp.int32, (N_EXPERTS, tm), 0)
    vals, ids = [], []
    for _ in range(TOP_K):
        m = jnp.max(logits, axis=0, keepdims=True)
        idx = jnp.min(jnp.where(logits == m, e_id, N_EXPERTS), axis=0, keepdims=True)
        vals.append(m)
        ids.append(idx)
        logits = jnp.where(e_id == idx, -jnp.inf, logits)
    exps = [jnp.exp(v - vals[0]) for v in vals]
    denom = exps[0] + exps[1] + exps[2] + exps[3]
    gate_ref[...] = jnp.concatenate([e / denom for e in exps], axis=0)
    te_ref[...] = jnp.concatenate(ids, axis=0)
    hits = [e_id == idx for idx in ids]
    picked = (hits[0] | hits[1] | hits[2] | hits[3])
    picked_f = jnp.where(picked, 1.0, 0.0)
    before = jnp.dot(picked_f.astype(BF16), tri_ref[...], preferred_element_type=F32) + carry_ref[...]
    ranks = [jnp.sum(jnp.where(hit, before, 0.0), axis=0, keepdims=True) for hit in hits]
    rank_ref[...] = jnp.concatenate(ranks, axis=0).astype(jnp.int32)
    carry_ref[...] = carry_ref[...] + jnp.sum(picked_f, axis=1, keepdims=True)
    cnt_ref[...] = jnp.broadcast_to(carry_ref[...], cnt_ref.shape)


def _route(x2, ret, moba, w_out, norm_w, w_router, b_router, tm):
    n = x2.shape[0]
    wo_r = w_out[:RET_V_W].astype(BF16)
    wo_m = w_out[RET_V_W:].astype(BF16)
    nw = norm_w.reshape(1, D_MODEL)
    wrt = w_router.T.astype(F32)
    br = b_router.reshape(N_EXPERTS, 1).astype(F32)
    tri = (np.arange(tm)[:, None] < np.arange(tm)[None, :]).astype(np.float32)
    tri = jnp.asarray(tri, dtype=BF16)
    row = lambda w: pl.BlockSpec((tm, w), lambda i: (i, 0))
    col = pl.BlockSpec((TOP_K, tm), lambda i: (0, i))
    full = lambda a: pl.BlockSpec(a.shape, lambda i: (0,) * a.ndim)
    out_shape = (
        jax.ShapeDtypeStruct((n, D_MODEL), F32),
        jax.ShapeDtypeStruct((n * ROW_TILES, LANES), F32),
        jax.ShapeDtypeStruct((TOP_K, n), jnp.int32),
        jax.ShapeDtypeStruct((TOP_K, n), F32),
        jax.ShapeDtypeStruct((TOP_K, n), jnp.int32),
        jax.ShapeDtypeStruct((N_EXPERTS, LANES), F32),
    )
    return pl.pallas_call(
        _route_kernel,
        grid=(n // tm,),
        in_specs=[row(D_MODEL), row(RET_V_W), row(MOBA_W), full(wo_r), full(wo_m), full(nw), full(wrt),
                  full(br), full(tri)],
        out_specs=(row(D_MODEL), pl.BlockSpec((tm * ROW_TILES, LANES), lambda i: (i, 0)), col, col, col,
                   pl.BlockSpec((N_EXPERTS, LANES), lambda i: (0, 0))),
        out_shape=out_shape,
        scratch_shapes=[pltpu.VMEM((N_EXPERTS, 1), F32)],
        compiler_params=_params(("arbitrary",)),
        name="out_proj_route",
    )(x2, ret, moba, wo_r, wo_m, nw, wrt, br, tri)


def _row_slice(ref, row):
    return ref.at[pl.ds(pl.multiple_of(row * ROW_TILES, ROW_TILES), ROW_TILES)]


DISPATCH_UNROLL = 4


def _dispatch_kernel(fill_lo_ref, fill_n_ref, dest_ref, h2_ref, xs_ref, zero_ref, sem, fill_sem, *, tt):
    @pl.when(pl.program_id(0) == 0)
    def _():
        zero_ref[...] = jnp.zeros_like(zero_ref)

        def fill_copy(e, r):
            return pltpu.make_async_copy(zero_ref, _row_slice(xs_ref, fill_lo_ref[e] + r), fill_sem)

        def fill(e, carry):
            def one(r, c):
                fill_copy(e, r).start()
                return c
            return lax.fori_loop(0, fill_n_ref[e], one, carry)

        def fill_wait(e, carry):
            def one(r, c):
                fill_copy(e, r).wait()
                return c
            return lax.fori_loop(0, fill_n_ref[e], one, carry)

        lax.fori_loop(0, N_EXPERTS, fill, 0)
        lax.fori_loop(0, N_EXPERTS, fill_wait, 0)

    def row_copy(t, k):
        return pltpu.make_async_copy(_row_slice(h2_ref, t), _row_slice(xs_ref, dest_ref[0, 0, k * tt + t]), sem)

    def issue(g, carry):
        for u in range(DISPATCH_UNROLL):
            for k in range(TOP_K):
                row_copy(g * DISPATCH_UNROLL + u, k).start(priority=k % 2)
        return carry

    lax.fori_loop(0, tt // DISPATCH_UNROLL, issue, 0)

    def drain(g, carry):
        for u in range(DISPATCH_UNROLL):
            for k in range(TOP_K):
                row_copy(g * DISPATCH_UNROLL + u, k).wait()
        return carry

    lax.fori_loop(0, tt // DISPATCH_UNROLL, drain, 0)


def _dispatch(h2_tiled, dest, fill_lo, fill_n, n_slots, tt):
    n = dest.shape[1]
    tiles = n // tt
    dest_t = dest.reshape(TOP_K, tiles, tt).transpose(1, 0, 2).reshape(tiles, 1, TOP_K * tt)
    grid_spec = pltpu.PrefetchScalarGridSpec(
        num_scalar_prefetch=2,
        grid=(tiles,),
        in_specs=[pl.BlockSpec((1, 1, TOP_K * tt), lambda i, lo, cnt: (i, 0, 0), memory_space=pltpu.SMEM),
                  pl.BlockSpec((tt * ROW_TILES, LANES), lambda i, lo, cnt: (i, 0))],
        out_specs=pl.BlockSpec(memory_space=pl.ANY),
        scratch_shapes=[pltpu.VMEM((ROW_TILES, LANES), F32), pltpu.SemaphoreType.DMA(()),
                        pltpu.SemaphoreType.DMA(())],
    )
    return pl.pallas_call(
        functools.partial(_dispatch_kernel, tt=tt),
        grid_spec=grid_spec,
        out_shape=jax.ShapeDtypeStruct((n_slots * ROW_TILES, LANES), F32),
        compiler_params=_params(("arbitrary",)),
        name="dispatch",
    )(fill_lo, fill_n, dest_t, h2_tiled)


def _expert_kernel(be_ref, used_ref, xs_ref, w1_ref, b1_ref, w2_ref, b2_ref, y_ref, *, mb):
    b = pl.program_id(0)

    @pl.when(b < used_ref[0])
    def _():
        x = _load_row_tiled(xs_ref, mb).astype(BF16)
        u = jnp.dot(x, w1_ref[0], preferred_element_type=F32) + b1_ref[0]
        gate = jnp.minimum(u[:, :D_EXPERT], SWIGLU_LIMIT)
        lin = jnp.clip(u[:, D_EXPERT:], -SWIGLU_LIMIT, SWIGLU_LIMIT)
        act = gate * jax.nn.sigmoid(SWIGLU_ALPHA * gate) * (lin + 1.0)
        y = jnp.dot(act.astype(BF16), w2_ref[0], preferred_element_type=F32) + b2_ref[0]
        _store_row_tiled(y_ref, y)

    @pl.when(b >= used_ref[0])
    def _():
        y_ref[...] = jnp.zeros_like(y_ref)


def _experts(xs, block_expert, n_used, w1, b1, w2, b2, mb):
    n_blocks = block_expert.shape[0]
    w1b = w1.astype(BF16)
    w2b = w2.astype(BF16)
    b1r = b1.reshape(N_EXPERTS, 1, 2 * D_EXPERT).astype(F32)
    b2r = b2.reshape(N_EXPERTS, 1, D_MODEL).astype(F32)
    grid_spec = pltpu.PrefetchScalarGridSpec(
        num_scalar_prefetch=2,
        grid=(n_blocks,),
        in_specs=[
            pl.BlockSpec((mb * ROW_TILES, LANES), lambda i, be, used: (jnp.minimum(i, used[0] - 1), 0)),
            pl.BlockSpec((1, D_MODEL, 2 * D_EXPERT), lambda i, be, used: (be[i], 0, 0)),
            pl.BlockSpec((1, 1, 2 * D_EXPERT), lambda i, be, used: (be[i], 0, 0)),
            pl.BlockSpec((1, D_EXPERT, D_MODEL), lambda i, be, used: (be[i], 0, 0)),
            pl.BlockSpec((1, 1, D_MODEL), lambda i, be, used: (be[i], 0, 0)),
        ],
        out_specs=pl.BlockSpec((mb * ROW_TILES, LANES), lambda i, be, used: (i, 0)),
    )
    return pl.pallas_call(
        functools.partial(_expert_kernel, mb=mb),
        grid_spec=grid_spec,
        out_shape=jax.ShapeDtypeStruct(xs.shape, F32),
        compiler_params=_params(("arbitrary",)),
        name="experts",
    )(block_expert, n_used, xs, w1b, b1r, w2b, b2r)


COMBINE_UNROLL = 4


def _combine_kernel(dest_ref, dest_next_ref, x1_ref, gate_ref, fw_ref, y_ref, o_ref, buf_ref, sems, *, tc, tiles):
    i = pl.program_id(0)
    slot = i % 2

    def row_copy(d_ref, dst_slot, t, k):
        return pltpu.make_async_copy(_row_slice(y_ref, d_ref[0, 0, k * tc + t]),
                                     _row_slice(buf_ref.at[dst_slot, k], t), sems.at[dst_slot])

    def gather(d_ref, dst_slot):
        def body(g, carry):
            for u in range(COMBINE_UNROLL):
                for k in range(TOP_K):
                    row_copy(d_ref, dst_slot, g * COMBINE_UNROLL + u, k).start(priority=k % 2)
            return carry
        lax.fori_loop(0, tc // COMBINE_UNROLL, body, 0)

    @pl.when(i == 0)
    def _():
        gather(dest_ref, 0)

    @pl.when(i + 1 < tiles)
    def _():
        gather(dest_next_ref, 1 - slot)

    def drain(g, carry):
        for u in range(COMBINE_UNROLL):
            for k in range(TOP_K):
                row_copy(dest_ref, slot, g * COMBINE_UNROLL + u, k).wait()
        return carry

    lax.fori_loop(0, tc // COMBINE_UNROLL, drain, 0)

    gates = gate_ref[...]
    gates_t = jnp.concatenate([gates, jnp.zeros((SUBLANES - TOP_K, tc), F32)], axis=0).T
    moe = None
    for k in range(TOP_K):
        part = gates_t[:, k:k + 1] * _load_row_tiled(buf_ref, tc, lead=(slot, k))
        moe = part if moe is None else moe + part
    xo = x1_ref[...] + moe
    ms = jnp.mean(xo * xo, axis=-1, keepdims=True)
    o_ref[...] = xo * lax.rsqrt(ms + RMS_EPS) * fw_ref[...]


def _combine(x1, gates, dest, y, final_w, tc):
    n = x1.shape[0]
    tiles = n // tc
    dest_t = dest.reshape(TOP_K, tiles, tc).transpose(1, 0, 2).reshape(tiles, 1, TOP_K * tc)
    fw = final_w.reshape(1, D_MODEL)
    dest_spec = lambda step: pl.BlockSpec((1, 1, TOP_K * tc), lambda i: (jnp.minimum(i + step, tiles - 1), 0, 0),
                                          memory_space=pltpu.SMEM)
    return pl.pallas_call(
        functools.partial(_combine_kernel, tc=tc, tiles=tiles),
        grid=(tiles,),
        in_specs=[dest_spec(0), dest_spec(1),
                  pl.BlockSpec((tc, D_MODEL), lambda i: (i, 0)),
                  pl.BlockSpec((TOP_K, tc), lambda i: (0, i)),
                  pl.BlockSpec((1, D_MODEL), lambda i: (0, 0)),
                  pl.BlockSpec(memory_space=pl.ANY)],
        out_specs=pl.BlockSpec((tc, D_MODEL), lambda i: (i, 0)),
        out_shape=jax.ShapeDtypeStruct((n, D_MODEL), F32),
        scratch_shapes=[pltpu.VMEM((2, TOP_K, tc * ROW_TILES, LANES), F32), pltpu.SemaphoreType.DMA((2,))],
        compiler_params=_params(("arbitrary",)),
        name="combine",
    )(dest_t, dest_t, x1, gates, fw, y)


PROJ_ROWS = 512
EXPERT_ROWS = 512
DISPATCH_ROWS = 512
COMBINE_ROWS = 256


def _slot_layout(counts, n_pairs, mb):
    padded = (counts + mb - 1) // mb * mb
    pad_end = jnp.cumsum(padded)
    pad_start = pad_end - padded
    n_blocks = n_pairs // mb + N_EXPERTS
    block_row0 = jnp.arange(n_blocks, dtype=jnp.int32) * mb
    block_expert = jnp.minimum(
        jnp.sum((pad_end[None, :] <= block_row0[:, None]).astype(jnp.int32), axis=1), N_EXPERTS - 1)
    n_used = (pad_end[-1:] // mb).astype(jnp.int32)
    return pad_start.astype(jnp.int32), (pad_start + counts).astype(jnp.int32), (padded - counts).astype(jnp.int32), \
        block_expert, n_used, n_blocks


def _layer(x2, batch, seq, norm1_w, w_in, ret_norm_w, w_out, norm2_w, w_router, b_router, w1, b1, w2, b2,
           final_w):
    n = x2.shape[0]
    tm = min(PROJ_ROWS, seq)
    rq, rk, rv, rg, mq, mk, mvt = _in_projection(x2, norm1_w, w_in, batch, seq, tm)
    ret = _retention(rq, rk, rv, rg, ret_norm_w, batch, seq)
    moba = _moba(mq, mk, mvt, batch, seq)
    x1, h2_tiled, top_e, gates, rank, cnt = _route(x2, ret, moba, w_out, norm2_w, w_router, b_router, tm)
    counts = cnt[:, 0].astype(jnp.int32)
    mb = EXPERT_ROWS
    pad_start, fill_lo, fill_n, block_expert, n_used, n_blocks = _slot_layout(counts, n * TOP_K, mb)
    expert_ids = jnp.arange(N_EXPERTS, dtype=jnp.int32)[:, None, None]
    dest = rank + jnp.sum(jnp.where(top_e[None] == expert_ids, pad_start[:, None, None], 0), axis=0)
    xs = _dispatch(h2_tiled, dest, fill_lo, fill_n, n_blocks * mb, min(DISPATCH_ROWS, n))
    y = _experts(xs, block_expert, n_used, w1, b1, w2, b2, mb)
    return _combine(x1, gates, dest, y, final_w, min(COMBINE_ROWS, n))


def kernel(x, norm1_w, w_in, ret_norm_w, w_out, norm2_w, w_router, b_router, w1, b1, w2, b2, final_norm_w):
    batch, seq, d = x.shape
    depth = w_in.shape[0]
    assert depth == 1 and d == D_MODEL and seq % MOBA_BLOCK == 0
    x2 = x.reshape(batch * seq, d)
    out = _layer(x2, batch, seq, norm1_w[0], w_in[0], ret_norm_w[0], w_out[0], norm2_w[0], w_router[0],
                 b_router[0], w1[0], b1[0], w2[0], b2[0], final_norm_w)
    return out.reshape(batch, seq, d)
```

```python
import functools

import numpy as np
import jax
import jax.numpy as jnp
from jax import lax
from jax.experimental import pallas as pl
from jax.experimental.pallas import tpu as pltpu

D_MODEL = 1024
RET_HEADS = 4
RET_DK = 64
RET_DV = 128
RET_CHUNK = 128
RET_THETA = 10000.0
MOBA_HEADS = 8
MOBA_DH = 64
MOBA_BLOCK = 256
MOBA_TOPK = 3
ROPE_THETA = 500000.0
ROPE_DIM = MOBA_DH // 4
RET_QK_W = RET_HEADS * RET_DK
RET_V_W = RET_HEADS * RET_DV
MOBA_W = MOBA_HEADS * MOBA_DH
N_EXPERTS = 32
TOP_K = 4
D_EXPERT = D_MODEL
SWIGLU_LIMIT = 7.0
SWIGLU_ALPHA = 1.702
RMS_EPS = 1e-5
GN_EPS = 1e-6
NEG_INF = -1e30

SUBLANES = 8
LANES = 128
ROW_TILES = D_MODEL // LANES
VMEM_LIMIT = 48 * 1024 * 1024

F32 = jnp.float32
BF16 = jnp.bfloat16
NT_DIMS = (((1,), (1,)), ((), ()))
TN_DIMS = (((0,), (0,)), ((), ()))


def _params(sem, vmem=VMEM_LIMIT, flags=None):
    return pltpu.CompilerParams(dimension_semantics=sem, vmem_limit_bytes=vmem, flags=flags)


def _rotate(x, cos, sin_lo, sin_hi, half):
    width = x.shape[-1]
    nxt = pltpu.roll(x, width - half, axis=1)
    prv = pltpu.roll(x, half, axis=1)
    return x * cos + nxt * sin_lo + prv * sin_hi


def _inproj_kernel(x_ref, nw_ref, w_ref, wvt_ref, rc_ref, rlo_ref, rhi_ref, mc_ref, mlo_ref, mhi_ref,
                   rq_ref, rk_ref, rv_ref, rg_ref, mq_ref, mk_ref, mvt_ref):
    x = x_ref[...]
    ms = jnp.mean(x * x, axis=-1, keepdims=True)
    h = (x * lax.rsqrt(ms + RMS_EPS) * nw_ref[...]).astype(BF16)
    proj = jnp.dot(h, w_ref[...], preferred_element_type=F32)
    o = 0
    rq = proj[:, o:o + RET_QK_W]; o += RET_QK_W
    rk = proj[:, o:o + RET_QK_W]; o += RET_QK_W
    rv = proj[:, o:o + RET_V_W]; o += RET_V_W
    rg = proj[:, o:o + RET_V_W]; o += RET_V_W
    mq = proj[:, o:o + MOBA_W]; o += MOBA_W
    mk = proj[:, o:o + MOBA_W]
    rc, rlo, rhi = rc_ref[...], rlo_ref[...], rhi_ref[...]
    mc, mlo, mhi = mc_ref[...], mlo_ref[...], mhi_ref[...]
    rq_ref[...] = _rotate(rq, rc, rlo, rhi, RET_DK // 2).astype(BF16)
    rk_ref[...] = _rotate(rk, rc, rlo, rhi, RET_DK // 2).astype(BF16)
    rv_ref[...] = rv.astype(BF16)
    rg_ref[...] = rg.astype(BF16)
    mq_ref[...] = (_rotate(mq, mc, mlo, mhi, ROPE_DIM // 2) * MOBA_DH ** -0.5).astype(BF16)
    mk_ref[...] = _rotate(mk, mc, mlo, mhi, ROPE_DIM // 2).astype(BF16)
    mvt_ref[0] = lax.dot_general(wvt_ref[...], h, NT_DIMS, preferred_element_type=F32).astype(BF16)


def _rotary_tables(seq, rot_dim, theta, head_dim, heads):
    half = rot_dim // 2
    inv_freq = (np.float32(theta) ** (np.float32(-2.0) * np.arange(half, dtype=np.float32) / np.float32(rot_dim)))
    ang = np.arange(seq, dtype=np.float32)[:, None] * inv_freq.astype(np.float32)[None, :]
    cos, sin = np.cos(ang).astype(np.float32), np.sin(ang).astype(np.float32)
    rest = head_dim - rot_dim
    ones = np.ones((seq, rest), np.float32)
    zeros = np.zeros((seq, rest), np.float32)
    zh = np.zeros((seq, half), np.float32)
    cos_t = np.concatenate([cos, cos, ones], axis=-1)
    lo_t = np.concatenate([-sin, zh, zeros], axis=-1)
    hi_t = np.concatenate([zh, sin, zeros], axis=-1)
    tile = lambda t: jnp.asarray(np.tile(t, (1, heads)))
    return tile(cos_t), tile(lo_t), tile(hi_t)


def _in_projection(x2, norm_w, w_in, batch, seq, tm):
    n = x2.shape[0]
    st = seq // tm
    w_main = w_in[:, :D_MODEL * 3 - MOBA_W].astype(BF16)
    w_vt = w_in[:, D_MODEL * 3 - MOBA_W:].T.astype(BF16)
    wm = w_main.shape[1]
    r_tabs = _rotary_tables(seq, RET_DK, RET_THETA, RET_DK, RET_HEADS)
    m_tabs = _rotary_tables(seq, ROPE_DIM, ROPE_THETA, MOBA_DH, MOBA_HEADS)
    row = lambda w: pl.BlockSpec((tm, w), lambda b, s: (b * st + s, 0))
    pos = lambda w: pl.BlockSpec((tm, w), lambda b, s: (s, 0))
    full = lambda a: pl.BlockSpec(a.shape, lambda b, s: (0,) * a.ndim)
    nw = norm_w.reshape(1, D_MODEL)
    out_shape = (
        jax.ShapeDtypeStruct((n, RET_QK_W), BF16), jax.ShapeDtypeStruct((n, RET_QK_W), BF16),
        jax.ShapeDtypeStruct((n, RET_V_W), BF16), jax.ShapeDtypeStruct((n, RET_V_W), BF16),
        jax.ShapeDtypeStruct((n, MOBA_W), BF16), jax.ShapeDtypeStruct((n, MOBA_W), BF16),
        jax.ShapeDtypeStruct((batch, MOBA_W, seq), BF16),
    )
    return pl.pallas_call(
        _inproj_kernel,
        grid=(batch, st),
        in_specs=[row(D_MODEL), full(nw), full(w_main), full(w_vt),
                  pos(RET_QK_W), pos(RET_QK_W), pos(RET_QK_W), pos(MOBA_W), pos(MOBA_W), pos(MOBA_W)],
        out_specs=(row(RET_QK_W), row(RET_QK_W), row(RET_V_W), row(RET_V_W), row(MOBA_W), row(MOBA_W),
                   pl.BlockSpec((1, MOBA_W, tm), lambda b, s: (b, 0, s))),
        out_shape=out_shape,
        compiler_params=_params(("parallel", "parallel")),
        name="in_projection",
    )(x2, nw, w_main, w_vt, *r_tabs, *m_tabs)


def _retention_kernel(q_ref, k_ref, v_ref, g_ref, decay_ref, xi_ref, zeta_ref, gnw_ref, o_ref, state_ref,
                      *, n_chunks, chunk_decay):
    c = RET_CHUNK
    state_ref[...] = jnp.zeros_like(state_ref)

    def chunk(ci, carry):
        r0 = pl.multiple_of(ci * c, c)
        q = q_ref[pl.ds(r0, c), :]
        k = k_ref[pl.ds(r0, c), :]
        v = v_ref[pl.ds(r0, c), :]
        g = g_ref[pl.ds(r0, c), :].astype(F32)
        qx = (q.astype(F32) * xi_ref[...]).astype(BF16)
        kz = (k.astype(F32) * zeta_ref[...]).astype(BF16)
        outs = []
        for h in range(RET_HEADS):
            qs = slice(h * RET_DK, (h + 1) * RET_DK)
            vs = slice(h * RET_DV, (h + 1) * RET_DV)
            vh = v[:, vs]
            scores = lax.dot_general(q[:, qs], k[:, qs], NT_DIMS, preferred_element_type=F32) * decay_ref[h]
            inner = jnp.dot(scores.astype(BF16), vh, preferred_element_type=F32)
            state = state_ref[h]
            cross = jnp.dot(qx[:, qs], state.astype(BF16), preferred_element_type=F32)
            state_ref[h] = chunk_decay[h] * state + lax.dot_general(
                kz[:, qs], vh, TN_DIMS, preferred_element_type=F32)
            ret = inner + cross
            mu = jnp.mean(ret, axis=-1, keepdims=True)
            cen = ret - mu
            var = jnp.mean(cen * cen, axis=-1, keepdims=True)
            outs.append(cen * lax.rsqrt(var + GN_EPS))
        normed = jnp.concatenate(outs, axis=-1) * gnw_ref[...]
        o_ref[pl.ds(r0, c), :] = (g * jax.nn.sigmoid(g) * normed).astype(BF16)
        return carry

    lax.fori_loop(0, n_chunks, chunk, 0)


def _retention_consts():
    h = np.arange(RET_HEADS, dtype=np.float32)
    log_g = np.log1p(-np.exp2(-5.0 - h)).astype(np.float32)
    idx = np.arange(RET_CHUNK, dtype=np.float32)
    rel = idx[:, None] - idx[None, :]
    scale = np.float32(RET_DK ** -0.5)
    decay = np.where(rel >= 0, np.exp(log_g[:, None, None] * np.maximum(rel, 0.0)), 0.0).astype(np.float32) * scale
    zeta = np.exp(log_g[:, None] * (RET_CHUNK - 1.0 - idx)).astype(np.float32) * scale
    xi = np.exp(log_g[:, None] * (idx + 1.0)).astype(np.float32)
    spread = lambda t: np.repeat(t.T, RET_DK, axis=1)
    chunk_decay = tuple(float(v) for v in np.exp(log_g * RET_CHUNK).astype(np.float32))
    return decay, spread(xi), spread(zeta), chunk_decay


def _retention(rq, rk, rv, rg, gn_w, batch, seq):
    n = rq.shape[0]
    decay, xi, zeta, chunk_decay = _retention_consts()
    gnw = gn_w.reshape(1, RET_V_W).astype(F32)
    seq_blk = lambda w: pl.BlockSpec((seq, w), lambda b: (b, 0))
    full = lambda a: pl.BlockSpec(a.shape, lambda b: (0,) * a.ndim)
    kern = functools.partial(_retention_kernel, n_chunks=seq // RET_CHUNK, chunk_decay=chunk_decay)
    return pl.pallas_call(
        kern,
        grid=(batch,),
        in_specs=[seq_blk(RET_QK_W), seq_blk(RET_QK_W), seq_blk(RET_V_W), seq_blk(RET_V_W),
                  full(decay), full(xi), full(zeta), full(gnw)],
        out_specs=seq_blk(RET_V_W),
        out_shape=jax.ShapeDtypeStruct((n, RET_V_W), BF16),
        scratch_shapes=[pltpu.VMEM((RET_HEADS, RET_DK, RET_DV), F32)],
        compiler_params=_params(("parallel",)),
        name="retention",
    )(rq, rk, rv, rg, jnp.asarray(decay), jnp.asarray(xi), jnp.asarray(zeta), gnw)


HEADS_PER_STEP = LANES // MOBA_DH


def _moba_kernel(q_ref, k_ref, vt_ref, o_ref, *, n_blocks):
    blk = MOBA_BLOCK
    dh = MOBA_DH
    blk_id = lax.broadcasted_iota(jnp.int32, (n_blocks, blk), 0)
    key_pos = lax.broadcasted_iota(jnp.int32, (blk, blk), 0)
    qry_pos = lax.broadcasted_iota(jnp.int32, (blk, blk), 1)
    causal = key_pos <= qry_pos
    rows = lambda b: slice(b * blk, (b + 1) * blk)

    def masked_scores(hh, j, k_mean):
        lanes = slice(hh * dh, (hh + 1) * dh)
        q = q_ref[rows(j), lanes]
        n_keys = (j + 1) * blk
        s_all = lax.dot_general(k_ref[0:n_keys, lanes], q, NT_DIMS, preferred_element_type=F32)
        if j > 0:
            gate = lax.dot_general(k_mean, q.astype(F32), NT_DIMS, preferred_element_type=F32)
            past = blk_id < j
            gate = jnp.where(past, gate, NEG_INF)
            rank = jnp.zeros((n_blocks, blk), jnp.int32)
            for other in range(j):
                g_o = gate[other:other + 1, :]
                ahead = (g_o > gate) | ((g_o == gate) & (other < blk_id))
                rank = rank + ahead.astype(jnp.int32)
            bias = jnp.where(past & (rank < MOBA_TOPK), 0.0, NEG_INF)
        s_blocks = [s_all[rows(i), :] + bias[i:i + 1, :] for i in range(j)]
        s_blocks.append(jnp.where(causal, s_all[rows(j), :], NEG_INF))
        m = jnp.max(s_blocks[0], axis=0, keepdims=True)
        for s in s_blocks[1:]:
            m = jnp.maximum(m, jnp.max(s, axis=0, keepdims=True))
        return s_blocks, m

    def softmax_weights(s_blocks, m):
        p_blocks = [jnp.exp(s - m) for s in s_blocks]
        l = jnp.sum(p_blocks[0], axis=0, keepdims=True)
        for p in p_blocks[1:]:
            l = l + jnp.sum(p, axis=0, keepdims=True)
        return jnp.concatenate([p.astype(BF16) for p in p_blocks], axis=0), l

    def weighted_values(hh, j, p_all, l):
        lanes = slice(hh * dh, (hh + 1) * dh)
        acc = jnp.dot(vt_ref[0, lanes, 0:(j + 1) * blk], p_all, preferred_element_type=F32)
        return (acc / l).T

    k_means = []
    for hh in range(HEADS_PER_STEP):
        kf = k_ref[:, hh * dh:(hh + 1) * dh].astype(F32)
        k_means.append(jnp.mean(kf.reshape(n_blocks, blk, dh), axis=1))
    units = [(j, hh) for j in range(n_blocks) for hh in range(HEADS_PER_STEP)]
    scored, weighted, outs = {}, {}, {}
    for step in range(len(units) + 2):
        if step < len(units):
            j, hh = units[step]
            scored[step] = masked_scores(hh, j, k_means[hh])
        if 0 <= step - 1 < len(units):
            weighted[step - 1] = softmax_weights(*scored.pop(step - 1))
        if 0 <= step - 2 < len(units):
            j, hh = units[step - 2]
            outs[(j, hh)] = weighted_values(hh, j, *weighted.pop(step - 2))
            if hh == HEADS_PER_STEP - 1:
                o_ref[rows(j), :] = jnp.concatenate(
                    [outs.pop((j, h)) for h in range(HEADS_PER_STEP)], axis=-1).astype(BF16)


def _moba(mq, mk, mvt, batch, seq):
    n = mq.shape[0]
    n_blocks = seq // MOBA_BLOCK
    pairs = MOBA_HEADS // HEADS_PER_STEP
    qk_spec = pl.BlockSpec((seq, LANES), lambda b, p: (b, p))
    kern = functools.partial(_moba_kernel, n_blocks=n_blocks)
    return pl.pallas_call(
        kern,
        grid=(batch, pairs),
        in_specs=[qk_spec, qk_spec, pl.BlockSpec((1, LANES, seq), lambda b, p: (b, p, 0))],
        out_specs=qk_spec,
        out_shape=jax.ShapeDtypeStruct((n, MOBA_W), BF16),
        compiler_params=_params(("parallel", "parallel")),
        name="moba",
    )(mq, mk, mvt)


def _store_row_tiled(ref, val):
    rows = val.shape[0]
    for s in range(ROW_TILES):
        ref[pl.ds(s, rows, stride=ROW_TILES), :] = val[:, s * LANES:(s + 1) * LANES]


def _load_row_tiled(ref, rows, lead=()):
    parts = []
    for s in range(ROW_TILES):
        parts.append(ref[tuple(lead) + (pl.ds(s, rows, stride=ROW_TILES), slice(None))])
    return jnp.concatenate(parts, axis=-1)


def _route_kernel(x_ref, ret_ref, moba_ref, wor_ref, wom_ref, nw_ref, wrt_ref, br_ref, tri_ref,
                  x1_ref, h2_ref, te_ref, gate_ref, rank_ref, cnt_ref, carry_ref):
    @pl.when(pl.program_id(0) == 0)
    def _():
        carry_ref[...] = jnp.zeros_like(carry_ref)

    x1 = (x_ref[...]
          + jnp.dot(ret_ref[...], wor_ref[...], preferred_element_type=F32)
          + jnp.dot(moba_ref[...], wom_ref[...], preferred_element_type=F32))
    x1_ref[...] = x1
    ms = jnp.mean(x1 * x1, axis=-1, keepdims=True)
    h2 = x1 * lax.rsqrt(ms + RMS_EPS) * nw_ref[...]
    _store_row_tiled(h2_ref, h2)
    logits = lax.dot_general(wrt_ref[...], h2, NT_DIMS, precision=lax.Precision.HIGHEST,
                             preferred_element_type=F32) + br_ref[...]
    tm = logits.shape[1]
    e_id = lax.broadcasted_iota(jnp.int32, (N_EXPERTS, tm), 0)
    vals, ids = [], []
    for _ in range(TOP_K):
        m = jnp.max(logits, axis=0, keepdims=True)
        idx = jnp.min(jnp.where(logits == m, e_id, N_EXPERTS), axis=0, keepdims=True)
        vals.append(m)
        ids.append(idx)
        logits = jnp.where(e_id == idx, -jnp.inf, logits)
    exps = [jnp.exp(v - vals[0]) for v in vals]
    denom = exps[0] + exps[1] + exps[2] + exps[3]
    gate_ref[...] = jnp.concatenate([e / denom for e in exps], axis=0)
    te_ref[...] = jnp.concatenate(ids, axis=0)
    hits = [e_id == idx for idx in ids]
    picked = (hits[0] | hits[1] | hits[2] | hits[3])
    picked_f = jnp.where(picked, 1.0, 0.0)
    before = jnp.dot(picked_f.astype(BF16), tri_ref[...], preferred_element_type=F32) + carry_ref[...]
    ranks = [jnp.sum(jnp.where(hit, before, 0.0), axis=0, keepdims=True) for hit in hits]
    rank_ref[...] = jnp.concatenate(ranks, axis=0).astype(jnp.int32)
    carry_ref[...] = carry_ref[...] + jnp.sum(picked_f, axis=1, keepdims=True)
    cnt_ref[...] = jnp.broadcast_to(carry_ref[...], cnt_ref.shape)


def _route(x2, ret, moba, w_out, norm_w, w_router, b_router, tm):
    n = x2.shape[0]
    wo_r = w_out[:RET_V_W].astype(BF16)
    wo_m = w_out[RET_V_W:].astype(BF16)
    nw = norm_w.reshape(1, D_MODEL)
    wrt = w_router.T.astype(F32)
    br = b_router.reshape(N_EXPERTS, 1).astype(F32)
    tri = (np.arange(tm)[:, None] < np.arange(tm)[None, :]).astype(np.float32)
    tri = jnp.asarray(tri, dtype=BF16)
    row = lambda w: pl.BlockSpec((tm, w), lambda i: (i, 0))
    col = pl.BlockSpec((TOP_K, tm), lambda i: (0, i))
    full = lambda a: pl.BlockSpec(a.shape, lambda i: (0,) * a.ndim)
    out_shape = (
        jax.ShapeDtypeStruct((n, D_MODEL), F32),
        jax.ShapeDtypeStruct((n * ROW_TILES, LANES), F32),
        jax.ShapeDtypeStruct((TOP_K, n), jnp.int32),
        jax.ShapeDtypeStruct((TOP_K, n), F32),
        jax.ShapeDtypeStruct((TOP_K, n), jnp.int32),
        jax.ShapeDtypeStruct((N_EXPERTS, LANES), F32),
    )
    return pl.pallas_call(
        _route_kernel,
        grid=(n // tm,),
        in_specs=[row(D_MODEL), row(RET_V_W), row(MOBA_W), full(wo_r), full(wo_m), full(nw), full(wrt),
                  full(br), full(tri)],
        out_specs=(row(D_MODEL), pl.BlockSpec((tm * ROW_TILES, LANES), lambda i: (i, 0)), col, col, col,
                   pl.BlockSpec((N_EXPERTS, LANES), lambda i: (0, 0))),
        out_shape=out_shape,
        scratch_shapes=[pltpu.VMEM((N_EXPERTS, 1), F32)],
        compiler_params=_params(("arbitrary",)),
        name="out_proj_route",
    )(x2, ret, moba, wo_r, wo_m, nw, wrt, br, tri)


def _row_slice(ref, row):
    return ref.at[pl.ds(pl.multiple_of(row * ROW_TILES, ROW_TILES), ROW_TILES)]


DISPATCH_UNROLL = 4


def _dispatch_kernel(fill_lo_ref, fill_n_ref, dest_ref, h2_ref, xs_ref, zero_ref, sem, fill_sem, *, tt):
    @pl.when(pl.program_id(0) == 0)
    def _():
        zero_ref[...] = jnp.zeros_like(zero_ref)

        def fill_copy(e, r):
            return pltpu.make_async_copy(zero_ref, _row_slice(xs_ref, fill_lo_ref[e] + r), fill_sem)

        def fill(e, carry):
            def one(r, c):
                fill_copy(e, r).start()
                return c
            return lax.fori_loop(0, fill_n_ref[e], one, carry)

        def fill_wait(e, carry):
            def one(r, c):
                fill_copy(e, r).wait()
                return c
            return lax.fori_loop(0, fill_n_ref[e], one, carry)

        lax.fori_loop(0, N_EXPERTS, fill, 0)
        lax.fori_loop(0, N_EXPERTS, fill_wait, 0)

    def row_copy(t, k):
        return pltpu.make_async_copy(_row_slice(h2_ref, t), _row_slice(xs_ref, dest_ref[0, 0, k * tt + t]), sem)

    def issue(g, carry):
        for u in range(DISPATCH_UNROLL):
            for k in range(TOP_K):
                row_copy(g * DISPATCH_UNROLL + u, k).start(priority=k % 2)
        return carry

    lax.fori_loop(0, tt // DISPATCH_UNROLL, issue, 0)

    def drain(g, carry):
        for u in range(DISPATCH_UNROLL):
            for k in range(TOP_K):
                row_copy(g * DISPATCH_UNROLL + u, k).wait()
        return carry

    lax.fori_loop(0, tt // DISPATCH_UNROLL, drain, 0)


def _dispatch(h2_tiled, dest, fill_lo, fill_n, n_slots, tt):
    n = dest.shape[1]
    tiles = n // tt
    dest_t = dest.reshape(TOP_K, tiles, tt).transpose(1, 0, 2).reshape(tiles, 1, TOP_K * tt)
    grid_spec = pltpu.PrefetchScalarGridSpec(
        num_scalar_prefetch=2,
        grid=(tiles,),
        in_specs=[pl.BlockSpec((1, 1, TOP_K * tt), lambda i, lo, cnt: (i, 0, 0), memory_space=pltpu.SMEM),
                  pl.BlockSpec((tt * ROW_TILES, LANES), lambda i, lo, cnt: (i, 0))],
        out_specs=pl.BlockSpec(memory_space=pl.ANY),
        scratch_shapes=[pltpu.VMEM((ROW_TILES, LANES), F32), pltpu.SemaphoreType.DMA(()),
                        pltpu.SemaphoreType.DMA(())],
    )
    return pl.pallas_call(
        functools.partial(_dispatch_kernel, tt=tt),
        grid_spec=grid_spec,
        out_shape=jax.ShapeDtypeStruct((n_slots * ROW_TILES, LANES), F32),
        compiler_params=_params(("arbitrary",)),
        name="dispatch",
    )(fill_lo, fill_n, dest_t, h2_tiled)


def _expert_kernel(be_ref, used_ref, xs_ref, w1_ref, b1_ref, w2_ref, b2_ref, y_ref, *, mb):
    b = pl.program_id(0)

    @pl.when(b < used_ref[0])
    def _():
        x = _load_row_tiled(xs_ref, mb).astype(BF16)
        u = jnp.dot(x, w1_ref[0], preferred_element_type=F32) + b1_ref[0]
        gate = jnp.minimum(u[:, :D_EXPERT], SWIGLU_LIMIT)
        lin = jnp.clip(u[:, D_EXPERT:], -SWIGLU_LIMIT, SWIGLU_LIMIT)
        act = gate * jax.nn.sigmoid(SWIGLU_ALPHA * gate) * (lin + 1.0)
        y = jnp.dot(act.astype(BF16), w2_ref[0], preferred_element_type=F32) + b2_ref[0]
        _store_row_tiled(y_ref, y)

    @pl.when(b >= used_ref[0])
    def _():
        y_ref[...] = jnp.zeros_like(y_ref)


def _experts(xs, block_expert, n_used, w1, b1, w2, b2, mb):
    n_blocks = block_expert.shape[0]
    w1b = w1.astype(BF16)
    w2b = w2.astype(BF16)
    b1r = b1.reshape(N_EXPERTS, 1, 2 * D_EXPERT).astype(F32)
    b2r = b2.reshape(N_EXPERTS, 1, D_MODEL).astype(F32)
    grid_spec = pltpu.PrefetchScalarGridSpec(
        num_scalar_prefetch=2,
        grid=(n_blocks,),
        in_specs=[
            pl.BlockSpec((mb * ROW_TILES, LANES), lambda i, be, used: (jnp.minimum(i, used[0] - 1), 0)),
            pl.BlockSpec((1, D_MODEL, 2 * D_EXPERT), lambda i, be, used: (be[i], 0, 0)),
            pl.BlockSpec((1, 1, 2 * D_EXPERT), lambda i, be, used: (be[i], 0, 0)),
            pl.BlockSpec((1, D_EXPERT, D_MODEL), lambda i, be, used: (be[i], 0, 0)),
            pl.BlockSpec((1, 1, D_MODEL), lambda i, be, used: (be[i], 0, 0)),
        ],
        out_specs=pl.BlockSpec((mb * ROW_TILES, LANES), lambda i, be, used: (i, 0)),
    )
    return pl.pallas_call(
        functools.partial(_expert_kernel, mb=mb),
        grid_spec=grid_spec,
        out_shape=jax.ShapeDtypeStruct(xs.shape, F32),
        compiler_params=_params(("arbitrary",)),
        name="experts",
    )(block_expert, n_used, xs, w1b, b1r, w2b, b2r)


COMBINE_UNROLL = 4
COMBINE_CHUNK = 64


def _combine_kernel(dest_ref, dest_next_ref, x1_ref, gate_ref, fw_ref, y_ref, o_ref, buf_ref, sems, *, tc, tiles):
    i = pl.program_id(0)
    slot = i % 2

    def row_copy(d_ref, dst_slot, t, k):
        return pltpu.make_async_copy(_row_slice(y_ref, d_ref[0, 0, k * tc + t]),
                                     _row_slice(buf_ref.at[dst_slot, k], t), sems.at[dst_slot])

    def gather(d_ref, dst_slot):
        def body(g, carry):
            for u in range(COMBINE_UNROLL):
                for k in range(TOP_K):
                    row_copy(d_ref, dst_slot, g * COMBINE_UNROLL + u, k).start(priority=k % 2)
            return carry
        lax.fori_loop(0, tc // COMBINE_UNROLL, body, 0)

    @pl.when(i == 0)
    def _():
        gather(dest_ref, 0)

    def drain(dst_slot):
        def body(g, carry):
            for u in range(COMBINE_UNROLL):
                for k in range(TOP_K):
                    row_copy(dest_ref, dst_slot, g * COMBINE_UNROLL + u, k).wait()
            return carry
        lax.fori_loop(0, tc // COMBINE_UNROLL, body, 0)

    drain(slot)

    gates = gate_ref[...]
    gates_t = jnp.concatenate([gates, jnp.zeros((SUBLANES - TOP_K, tc), F32)], axis=0).T
    pieces = (tc // COMBINE_CHUNK) * ROW_TILES
    per_piece = tc // pieces
    piece = 0
    for c in range(tc // COMBINE_CHUNK):
        tok = slice(c * COMBINE_CHUNK, (c + 1) * COMBINE_CHUNK)
        gate_cols = [jnp.broadcast_to(gates_t[tok, k:k + 1], (COMBINE_CHUNK, LANES)) for k in range(TOP_K)]
        sum_sq = jnp.zeros((COMBINE_CHUNK, 1), F32)
        for s in range(ROW_TILES):
            slab = slice(s * LANES, (s + 1) * LANES)
            xo = x1_ref[tok, slab]
            for k in range(TOP_K):
                rows = pl.ds(c * COMBINE_CHUNK * ROW_TILES + s, COMBINE_CHUNK, stride=ROW_TILES)
                xo = xo + gate_cols[k] * buf_ref[slot, k, rows, :]
            o_ref[tok, slab] = xo
            sum_sq = sum_sq + jnp.sum(xo * xo, axis=-1, keepdims=True)
            for t in range(piece * per_piece, (piece + 1) * per_piece):
                for k in range(TOP_K):
                    row_copy(dest_next_ref, 1 - slot, t, k).start(priority=k % 2)
            piece += 1
        inv = lax.rsqrt(sum_sq * (1.0 / D_MODEL) + RMS_EPS)
        o_ref[tok, :] = o_ref[tok, :] * inv * fw_ref[...]

    @pl.when(i == tiles - 1)
    def _():
        drain(1 - slot)


def _combine(x1, gates, dest, y, final_w, tc):
    n = x1.shape[0]
    tiles = n // tc
    dest_t = dest.reshape(TOP_K, tiles, tc).transpose(1, 0, 2).reshape(tiles, 1, TOP_K * tc)
    fw = final_w.reshape(1, D_MODEL)
    dest_spec = lambda step: pl.BlockSpec((1, 1, TOP_K * tc), lambda i: (jnp.minimum(i + step, tiles - 1), 0, 0),
                                          memory_space=pltpu.SMEM)
    return pl.pallas_call(
        functools.partial(_combine_kernel, tc=tc, tiles=tiles),
        grid=(tiles,),
        in_specs=[dest_spec(0), dest_spec(1),
                  pl.BlockSpec((tc, D_MODEL), lambda i: (i, 0)),
                  pl.BlockSpec((TOP_K, tc), lambda i: (0, i)),
                  pl.BlockSpec((1, D_MODEL), lambda i: (0, 0)),
                  pl.BlockSpec(memory_space=pl.ANY)],
        out_specs=pl.BlockSpec((tc, D_MODEL), lambda i: (i, 0)),
        out_shape=jax.ShapeDtypeStruct((n, D_MODEL), F32),
        scratch_shapes=[pltpu.VMEM((2, TOP_K, tc * ROW_TILES, LANES), F32), pltpu.SemaphoreType.DMA((2,))],
        compiler_params=_params(("arbitrary",)),
        name="combine",
    )(dest_t, dest_t, x1, gates, fw, y)


PROJ_ROWS = 512
EXPERT_ROWS = 512
DISPATCH_ROWS = 512
COMBINE_ROWS = 256


def _slot_layout(counts, n_pairs, mb):
    padded = (counts + mb - 1) // mb * mb
    pad_end = jnp.cumsum(padded)
    pad_start = pad_end - padded
    n_blocks = n_pairs // mb + N_EXPERTS
    block_row0 = jnp.arange(n_blocks, dtype=jnp.int32) * mb
    block_expert = jnp.minimum(
        jnp.sum((pad_end[None, :] <= block_row0[:, None]).astype(jnp.int32), axis=1), N_EXPERTS - 1)
    n_used = (pad_end[-1:] // mb).astype(jnp.int32)
    return pad_start.astype(jnp.int32), (pad_start + counts).astype(jnp.int32), (padded - counts).astype(jnp.int32), \
        block_expert, n_used, n_blocks


def _layer(x2, batch, seq, norm1_w, w_in, ret_norm_w, w_out, norm2_w, w_router, b_router, w1, b1, w2, b2,
           final_w):
    n = x2.shape[0]
    tm = min(PROJ_ROWS, seq)
    rq, rk, rv, rg, mq, mk, mvt = _in_projection(x2, norm1_w, w_in, batch, seq, tm)
    ret = _retention(rq, rk, rv, rg, ret_norm_w, batch, seq)
    moba = _moba(mq, mk, mvt, batch, seq)
    x1, h2_tiled, top_e, gates, rank, cnt = _route(x2, ret, moba, w_out, norm2_w, w_router, b_router, tm)
    counts = cnt[:, 0].astype(jnp.int32)
    mb = EXPERT_ROWS
    pad_start, fill_lo, fill_n, block_expert, n_used, n_blocks = _slot_layout(counts, n * TOP_K, mb)
    expert_ids = jnp.arange(N_EXPERTS, dtype=jnp.int32)[:, None, None]
    dest = rank + jnp.sum(jnp.where(top_e[None] == expert_ids, pad_start[:, None, None], 0), axis=0)
    xs = _dispatch(h2_tiled, dest, fill_lo, fill_n, n_blocks * mb, min(DISPATCH_ROWS, n))
    y = _experts(xs, block_expert, n_used, w1, b1, w2, b2, mb)
    return _combine(x1, gates, dest, y, final_w, min(COMBINE_ROWS, n))


def kernel(x, norm1_w, w_in, ret_norm_w, w_out, norm2_w, w_router, b_router, w1, b1, w2, b2, final_norm_w):
    batch, seq, d = x.shape
    depth = w_in.shape[0]
    assert depth == 1 and d == D_MODEL and seq % MOBA_BLOCK == 0
    x2 = x.reshape(batch * seq, d)
    out = _layer(x2, batch, seq, norm1_w[0], w_in[0], ret_norm_w[0], w_out[0], norm2_w[0], w_router[0],
                 b_router[0], w1[0], b1[0], w2[0], b2[0], final_norm_w)
    return out.reshape(batch, seq, d)
```

```python
import functools

import numpy as np
import jax
import jax.numpy as jnp
from jax import lax
from jax.experimental import pallas as pl
from jax.experimental.pallas import tpu as pltpu

D_MODEL = 1024
RET_HEADS = 4
RET_DK = 64
RET_DV = 128
RET_CHUNK = 128
RET_THETA = 10000.0
MOBA_HEADS = 8
MOBA_DH = 64
MOBA_BLOCK = 256
MOBA_TOPK = 3
ROPE_THETA = 500000.0
ROPE_DIM = MOBA_DH // 4
RET_QK_W = RET_HEADS * RET_DK
RET_V_W = RET_HEADS * RET_DV
MOBA_W = MOBA_HEADS * MOBA_DH
N_EXPERTS = 32
TOP_K = 4
D_EXPERT = D_MODEL
SWIGLU_LIMIT = 7.0
SWIGLU_ALPHA = 1.702
RMS_EPS = 1e-5
GN_EPS = 1e-6
NEG_INF = -1e30

SUBLANES = 8
LANES = 128
ROW_TILES = D_MODEL // LANES
VMEM_LIMIT = 48 * 1024 * 1024
EXPERT_VMEM_LIMIT = 56 * 1024 * 1024

F32 = jnp.float32
BF16 = jnp.bfloat16
NT_DIMS = (((1,), (1,)), ((), ()))
TN_DIMS = (((0,), (0,)), ((), ()))


def _params(sem, vmem=VMEM_LIMIT, flags=None):
    return pltpu.CompilerParams(dimension_semantics=sem, vmem_limit_bytes=vmem, flags=flags)


def _rotate(x, cos, sin_lo, sin_hi, half):
    width = x.shape[-1]
    nxt = pltpu.roll(x, width - half, axis=1)
    prv = pltpu.roll(x, half, axis=1)
    return x * cos + nxt * sin_lo + prv * sin_hi


def _inproj_kernel(x_ref, nw_ref, w_ref, wvt_ref, rc_ref, rlo_ref, rhi_ref, mc_ref, mlo_ref, mhi_ref,
                   rq_ref, rk_ref, rv_ref, rg_ref, mq_ref, mk_ref, mvt_ref):
    x = x_ref[...]
    ms = jnp.mean(x * x, axis=-1, keepdims=True)
    h = (x * lax.rsqrt(ms + RMS_EPS) * nw_ref[...]).astype(BF16)
    proj = jnp.dot(h, w_ref[...], preferred_element_type=F32)
    o = 0
    rq = proj[:, o:o + RET_QK_W]; o += RET_QK_W
    rk = proj[:, o:o + RET_QK_W]; o += RET_QK_W
    rv = proj[:, o:o + RET_V_W]; o += RET_V_W
    rg = proj[:, o:o + RET_V_W]; o += RET_V_W
    mq = proj[:, o:o + MOBA_W]; o += MOBA_W
    mk = proj[:, o:o + MOBA_W]
    rc, rlo, rhi = rc_ref[...], rlo_ref[...], rhi_ref[...]
    mc, mlo, mhi = mc_ref[...], mlo_ref[...], mhi_ref[...]
    rq_ref[...] = _rotate(rq, rc, rlo, rhi, RET_DK // 2).astype(BF16)
    rk_ref[...] = _rotate(rk, rc, rlo, rhi, RET_DK // 2).astype(BF16)
    rv_ref[...] = rv.astype(BF16)
    rg_ref[...] = rg.astype(BF16)
    mq_ref[...] = (_rotate(mq, mc, mlo, mhi, ROPE_DIM // 2) * MOBA_Q_SCALE).astype(BF16)
    mk_ref[...] = _rotate(mk, mc, mlo, mhi, ROPE_DIM // 2).astype(BF16)
    mvt_ref[0] = lax.dot_general(wvt_ref[...], h, NT_DIMS, preferred_element_type=F32).astype(BF16)


def _rotary_tables(seq, rot_dim, theta, head_dim, heads):
    half = rot_dim // 2
    inv_freq = (np.float32(theta) ** (np.float32(-2.0) * np.arange(half, dtype=np.float32) / np.float32(rot_dim)))
    ang = np.arange(seq, dtype=np.float32)[:, None] * inv_freq.astype(np.float32)[None, :]
    cos, sin = np.cos(ang).astype(np.float32), np.sin(ang).astype(np.float32)
    rest = head_dim - rot_dim
    ones = np.ones((seq, rest), np.float32)
    zeros = np.zeros((seq, rest), np.float32)
    zh = np.zeros((seq, half), np.float32)
    cos_t = np.concatenate([cos, cos, ones], axis=-1)
    lo_t = np.concatenate([-sin, zh, zeros], axis=-1)
    hi_t = np.concatenate([zh, sin, zeros], axis=-1)
    tile = lambda t: jnp.asarray(np.tile(t, (1, heads)))
    return tile(cos_t), tile(lo_t), tile(hi_t)


def _in_projection(x2, norm_w, w_in, batch, seq, tm):
    n = x2.shape[0]
    st = seq // tm
    w_main = w_in[:, :D_MODEL * 3 - MOBA_W].astype(BF16)
    w_vt = w_in[:, D_MODEL * 3 - MOBA_W:].T.astype(BF16)
    wm = w_main.shape[1]
    r_tabs = _rotary_tables(seq, RET_DK, RET_THETA, RET_DK, RET_HEADS)
    m_tabs = _rotary_tables(seq, ROPE_DIM, ROPE_THETA, MOBA_DH, MOBA_HEADS)
    row = lambda w: pl.BlockSpec((tm, w), lambda b, s: (b * st + s, 0))
    pos = lambda w: pl.BlockSpec((tm, w), lambda b, s: (s, 0))
    full = lambda a: pl.BlockSpec(a.shape, lambda b, s: (0,) * a.ndim)
    nw = norm_w.reshape(1, D_MODEL)
    out_shape = (
        jax.ShapeDtypeStruct((n, RET_QK_W), BF16), jax.ShapeDtypeStruct((n, RET_QK_W), BF16),
        jax.ShapeDtypeStruct((n, RET_V_W), BF16), jax.ShapeDtypeStruct((n, RET_V_W), BF16),
        jax.ShapeDtypeStruct((n, MOBA_W), BF16), jax.ShapeDtypeStruct((n, MOBA_W), BF16),
        jax.ShapeDtypeStruct((batch, MOBA_W, seq), BF16),
    )
    return pl.pallas_call(
        _inproj_kernel,
        grid=(batch, st),
        in_specs=[row(D_MODEL), full(nw), full(w_main), full(w_vt),
                  pos(RET_QK_W), pos(RET_QK_W), pos(RET_QK_W), pos(MOBA_W), pos(MOBA_W), pos(MOBA_W)],
        out_specs=(row(RET_QK_W), row(RET_QK_W), row(RET_V_W), row(RET_V_W), row(MOBA_W), row(MOBA_W),
                   pl.BlockSpec((1, MOBA_W, tm), lambda b, s: (b, 0, s))),
        out_shape=out_shape,
        compiler_params=_params(("parallel", "parallel")),
        name="in_projection",
    )(x2, nw, w_main, w_vt, *r_tabs, *m_tabs)


RET_UNROLL = 4


def _retention_kernel(q_ref, k_ref, v_ref, g_ref, decay_ref, xi_ref, zeta_ref, gnw_ref, o_ref, state_ref,
                      *, n_chunks, chunk_decay):
    c = RET_CHUNK
    state_ref[...] = jnp.zeros_like(state_ref)

    def chunk_group(gi, carry):
        states = [state_ref[h] for h in range(RET_HEADS)]
        for u in range(RET_UNROLL):
            r0 = pl.multiple_of((gi * RET_UNROLL + u) * c, c)
            q = q_ref[pl.ds(r0, c), :]
            k = k_ref[pl.ds(r0, c), :]
            v = v_ref[pl.ds(r0, c), :]
            g = g_ref[pl.ds(r0, c), :].astype(F32)
            qx = (q.astype(F32) * xi_ref[...]).astype(BF16)
            kz = (k.astype(F32) * zeta_ref[...]).astype(BF16)
            outs = []
            for h in range(RET_HEADS):
                qs = slice(h * RET_DK, (h + 1) * RET_DK)
                vs = slice(h * RET_DV, (h + 1) * RET_DV)
                vh = v[:, vs]
                scores = lax.dot_general(q[:, qs], k[:, qs], NT_DIMS, preferred_element_type=F32) * decay_ref[h]
                inner = jnp.dot(scores.astype(BF16), vh, preferred_element_type=F32)
                cross = jnp.dot(qx[:, qs], states[h].astype(BF16), preferred_element_type=F32)
                states[h] = chunk_decay[h] * states[h] + lax.dot_general(
                    kz[:, qs], vh, TN_DIMS, preferred_element_type=F32)
                ret = inner + cross
                mu = jnp.mean(ret, axis=-1, keepdims=True)
                cen = ret - mu
                var = jnp.mean(cen * cen, axis=-1, keepdims=True)
                outs.append(cen * lax.rsqrt(var + GN_EPS))
            normed = jnp.concatenate(outs, axis=-1) * gnw_ref[...]
            o_ref[pl.ds(r0, c), :] = (g * jax.nn.sigmoid(g) * normed).astype(BF16)
        for h in range(RET_HEADS):
            state_ref[h] = states[h]
        return carry

    lax.fori_loop(0, n_chunks // RET_UNROLL, chunk_group, 0)


def _retention_consts():
    h = np.arange(RET_HEADS, dtype=np.float32)
    log_g = np.log1p(-np.exp2(-5.0 - h)).astype(np.float32)
    idx = np.arange(RET_CHUNK, dtype=np.float32)
    rel = idx[:, None] - idx[None, :]
    scale = np.float32(RET_DK ** -0.5)
    decay = np.where(rel >= 0, np.exp(log_g[:, None, None] * np.maximum(rel, 0.0)), 0.0).astype(np.float32) * scale
    zeta = np.exp(log_g[:, None] * (RET_CHUNK - 1.0 - idx)).astype(np.float32) * scale
    xi = np.exp(log_g[:, None] * (idx + 1.0)).astype(np.float32)
    spread = lambda t: np.repeat(t.T, RET_DK, axis=1)
    chunk_decay = tuple(float(v) for v in np.exp(log_g * RET_CHUNK).astype(np.float32))
    return decay, spread(xi), spread(zeta), chunk_decay


def _retention(rq, rk, rv, rg, gn_w, batch, seq):
    n = rq.shape[0]
    decay, xi, zeta, chunk_decay = _retention_consts()
    gnw = gn_w.reshape(1, RET_V_W).astype(F32)
    seq_blk = lambda w: pl.BlockSpec((seq, w), lambda b: (b, 0))
    full = lambda a: pl.BlockSpec(a.shape, lambda b: (0,) * a.ndim)
    kern = functools.partial(_retention_kernel, n_chunks=seq // RET_CHUNK, chunk_decay=chunk_decay)
    return pl.pallas_call(
        kern,
        grid=(batch,),
        in_specs=[seq_blk(RET_QK_W), seq_blk(RET_QK_W), seq_blk(RET_V_W), seq_blk(RET_V_W),
                  full(decay), full(xi), full(zeta), full(gnw)],
        out_specs=seq_blk(RET_V_W),
        out_shape=jax.ShapeDtypeStruct((n, RET_V_W), BF16),
        scratch_shapes=[pltpu.VMEM((RET_HEADS, RET_DK, RET_DV), F32)],
        compiler_params=_params(("parallel",)),
        name="retention",
    )(rq, rk, rv, rg, jnp.asarray(decay), jnp.asarray(xi), jnp.asarray(zeta), gnw)


HEADS_PER_STEP = LANES // MOBA_DH
MOBA_Q_SCALE = MOBA_DH ** -0.5 * float(np.log2(np.e))


def _moba_kernel(q_ref, k_ref, vt_ref, o_ref, *, n_blocks):
    blk = MOBA_BLOCK
    dh = MOBA_DH
    blk_id = lax.broadcasted_iota(jnp.int32, (n_blocks, blk), 0)
    key_pos = lax.broadcasted_iota(jnp.int32, (blk, blk), 0)
    qry_pos = lax.broadcasted_iota(jnp.int32, (blk, blk), 1)
    causal = key_pos <= qry_pos
    rows = lambda b: slice(b * blk, (b + 1) * blk)

    def masked_scores(hh, j, k_mean):
        lanes = slice(hh * dh, (hh + 1) * dh)
        q = q_ref[rows(j), lanes]
        n_keys = (j + 1) * blk
        s_all = lax.dot_general(k_ref[0:n_keys, lanes], q, NT_DIMS, preferred_element_type=F32)
        if j > 0:
            gate = lax.dot_general(k_mean, q.astype(F32), NT_DIMS, preferred_element_type=F32)
            past = blk_id < j
            gate = jnp.where(past, gate, NEG_INF)
            rank = jnp.zeros((n_blocks, blk), jnp.int32)
            for other in range(j):
                g_o = gate[other:other + 1, :]
                ahead = (g_o > gate) | ((g_o == gate) & (other < blk_id))
                rank = rank + ahead.astype(jnp.int32)
            bias = jnp.where(past & (rank < MOBA_TOPK), 0.0, NEG_INF)
        s_blocks = [s_all[rows(i), :] + bias[i:i + 1, :] for i in range(j)]
        s_blocks.append(jnp.where(causal, s_all[rows(j), :], NEG_INF))
        m = jnp.max(s_blocks[0], axis=0, keepdims=True)
        for s in s_blocks[1:]:
            m = jnp.maximum(m, jnp.max(s, axis=0, keepdims=True))
        return s_blocks, m

    def softmax_weights(s_blocks, m):
        p_blocks = [jnp.exp2(s - m) for s in s_blocks]
        l = jnp.sum(p_blocks[0], axis=0, keepdims=True)
        for p in p_blocks[1:]:
            l = l + jnp.sum(p, axis=0, keepdims=True)
        return jnp.concatenate([p.astype(BF16) for p in p_blocks], axis=0), l

    def weighted_values(hh, j, p_all, l):
        lanes = slice(hh * dh, (hh + 1) * dh)
        acc = jnp.dot(vt_ref[0, lanes, 0:(j + 1) * blk], p_all, preferred_element_type=F32)
        return (acc / l).T

    k_means = []
    for hh in range(HEADS_PER_STEP):
        kf = k_ref[:, hh * dh:(hh + 1) * dh].astype(F32)
        k_means.append(jnp.mean(kf.reshape(n_blocks, blk, dh), axis=1))
    units = [(j, hh) for j in range(n_blocks) for hh in range(HEADS_PER_STEP)]
    scored, weighted, outs = {}, {}, {}
    for step in range(len(units) + 2):
        if step < len(units):
            j, hh = units[step]
            scored[step] = masked_scores(hh, j, k_means[hh])
        if 0 <= step - 1 < len(units):
            weighted[step - 1] = softmax_weights(*scored.pop(step - 1))
        if 0 <= step - 2 < len(units):
            j, hh = units[step - 2]
            outs[(j, hh)] = weighted_values(hh, j, *weighted.pop(step - 2))
            if hh == HEADS_PER_STEP - 1:
                o_ref[rows(j), :] = jnp.concatenate(
                    [outs.pop((j, h)) for h in range(HEADS_PER_STEP)], axis=-1).astype(BF16)


def _moba(mq, mk, mvt, batch, seq):
    n = mq.shape[0]
    n_blocks = seq // MOBA_BLOCK
    pairs = MOBA_HEADS // HEADS_PER_STEP
    qk_spec = pl.BlockSpec((seq, LANES), lambda b, p: (b, p))
    kern = functools.partial(_moba_kernel, n_blocks=n_blocks)
    return pl.pallas_call(
        kern,
        grid=(batch, pairs),
        in_specs=[qk_spec, qk_spec, pl.BlockSpec((1, LANES, seq), lambda b, p: (b, p, 0))],
        out_specs=qk_spec,
        out_shape=jax.ShapeDtypeStruct((n, MOBA_W), BF16),
        compiler_params=_params(("parallel", "parallel")),
        name="moba",
    )(mq, mk, mvt)


def _store_row_tiled(ref, val):
    rows = val.shape[0]
    for s in range(ROW_TILES):
        ref[pl.ds(s, rows, stride=ROW_TILES), :] = val[:, s * LANES:(s + 1) * LANES]


def _load_row_tiled(ref, rows, lead=()):
    parts = []
    for s in range(ROW_TILES):
        parts.append(ref[tuple(lead) + (pl.ds(s, rows, stride=ROW_TILES), slice(None))])
    return jnp.concatenate(parts, axis=-1)


def _route_kernel(x_ref, ret_ref, moba_ref, wor_ref, wom_ref, nw_ref, wrt_ref, br_ref, tri_ref,
                  x1_ref, h2_ref, te_ref, gate_ref, rank_ref, cnt_ref, carry_ref):
    @pl.when(pl.program_id(0) == 0)
    def _():
        carry_ref[...] = jnp.zeros_like(carry_ref)

    x1 = (x_ref[...]
          + jnp.dot(ret_ref[...], wor_ref[...], preferred_element_type=F32)
          + jnp.dot(moba_ref[...], wom_ref[...], preferred_element_type=F32))
    x1_ref[...] = x1
    ms = jnp.mean(x1 * x1, axis=-1, keepdims=True)
    h2 = x1 * lax.rsqrt(ms + RMS_EPS) * nw_ref[...]
    _store_row_tiled(h2_ref, h2)
    logits = lax.dot_general(wrt_ref[...], h2, NT_DIMS, precision=lax.Precision.HIGHEST,
                             preferred_element_type=F32) + br_ref[...]
    tm = logits.shape[1]
    e_id = lax.broadcasted_iota(jnp.int32, (N_EXPERTS, tm), 0)
    vals, ids = [], []
    for _ in range(TOP_K):
        m = jnp.max(logits, axis=0, keepdims=True)
        idx = jnp.min(jnp.where(logits == m, e_id, N_EXPERTS), axis=0, keepdims=True)
        vals.append(m)
        ids.append(idx)
        logits = jnp.where(e_id == idx, -jnp.inf, logits)
    exps = [jnp.exp(v - vals[0]) for v in vals]
    denom = exps[0] + exps[1] + exps[2] + exps[3]
    gate_ref[...] = jnp.concatenate([e / denom for e in exps], axis=0)
    te_ref[...] = jnp.concatenate(ids, axis=0)
    hits = [e_id == idx for idx in ids]
    picked = (hits[0] | hits[1] | hits[2] | hits[3])
    picked_f = jnp.where(picked, 1.0, 0.0)
    before = jnp.dot(picked_f.astype(BF16), tri_ref[...], preferred_element_type=F32) + carry_ref[...]
    ranks = [jnp.sum(jnp.where(hit, before, 0.0), axis=0, keepdims=True) for hit in hits]
    rank_ref[...] = jnp.concatenate(ranks, axis=0).astype(jnp.int32)
    carry_ref[...] = carry_ref[...] + jnp.sum(picked_f, axis=1, keepdims=True)
    cnt_ref[...] = jnp.broadcast_to(carry_ref[...], cnt_ref.shape)


def _route(x2, ret, moba, w_out, norm_w, w_router, b_router, tm):
    n = x2.shape[0]
    wo_r = w_out[:RET_V_W].astype(BF16)
    wo_m = w_out[RET_V_W:].astype(BF16)
    nw = norm_w.reshape(1, D_MODEL)
    wrt = w_router.T.astype(F32)
    br = b_router.reshape(N_EXPERTS, 1).astype(F32)
    tri = (np.arange(tm)[:, None] < np.arange(tm)[None, :]).astype(np.float32)
    tri = jnp.asarray(tri, dtype=BF16)
    row = lambda w: pl.BlockSpec((tm, w), lambda i: (i, 0))
    col = pl.BlockSpec((TOP_K, tm), lambda i: (0, i))
    full = lambda a: pl.BlockSpec(a.shape, lambda i: (0,) * a.ndim)
    out_shape = (
        jax.ShapeDtypeStruct((n, D_MODEL), F32),
        jax.ShapeDtypeStruct((n * ROW_TILES, LANES), F32),
        jax.ShapeDtypeStruct((TOP_K, n), jnp.int32),
        jax.ShapeDtypeStruct((TOP_K, n), F32),
        jax.ShapeDtypeStruct((TOP_K, n), jnp.int32),
        jax.ShapeDtypeStruct((N_EXPERTS, LANES), F32),
    )
    return pl.pallas_call(
        _route_kernel,
        grid=(n // tm,),
        in_specs=[row(D_MODEL), row(RET_V_W), row(MOBA_W), full(wo_r), full(wo_m), full(nw), full(wrt),
                  full(br), full(tri)],
        out_specs=(row(D_MODEL), pl.BlockSpec((tm * ROW_TILES, LANES), lambda i: (i, 0)), col, col, col,
                   pl.BlockSpec((N_EXPERTS, LANES), lambda i: (0, 0))),
        out_shape=out_shape,
        scratch_shapes=[pltpu.VMEM((N_EXPERTS, 1), F32)],
        compiler_params=_params(("arbitrary",)),
        name="out_proj_route",
    )(x2, ret, moba, wo_r, wo_m, nw, wrt, br, tri)


def _row_slice(ref, row):
    return ref.at[pl.ds(pl.multiple_of(row * ROW_TILES, ROW_TILES), ROW_TILES)]


DISPATCH_UNROLL = 4


def _dispatch_kernel(fill_lo_ref, fill_n_ref, dest_ref, h2_ref, xs_ref, zero_ref, sem, fill_sem, *, tt):
    @pl.when(pl.program_id(0) == 0)
    def _():
        zero_ref[...] = jnp.zeros_like(zero_ref)

        def fill_copy(e, r):
            return pltpu.make_async_copy(zero_ref, _row_slice(xs_ref, fill_lo_ref[e] + r), fill_sem)

        def fill(e, carry):
            def one(r, c):
                fill_copy(e, r).start()
                return c
            return lax.fori_loop(0, fill_n_ref[e], one, carry)

        def fill_wait(e, carry):
            def one(r, c):
                fill_copy(e, r).wait()
                return c
            return lax.fori_loop(0, fill_n_ref[e], one, carry)

        lax.fori_loop(0, N_EXPERTS, fill, 0)
        lax.fori_loop(0, N_EXPERTS, fill_wait, 0)

    def row_copy(t, k):
        return pltpu.make_async_copy(_row_slice(h2_ref, t), _row_slice(xs_ref, dest_ref[0, 0, k * tt + t]), sem)

    def issue(g, carry):
        for u in range(DISPATCH_UNROLL):
            for k in range(TOP_K):
                row_copy(g * DISPATCH_UNROLL + u, k).start(priority=k % 2)
        return carry

    lax.fori_loop(0, tt // DISPATCH_UNROLL, issue, 0)

    def drain(g, carry):
        for u in range(DISPATCH_UNROLL):
            for k in range(TOP_K):
                row_copy(g * DISPATCH_UNROLL + u, k).wait()
        return carry

    lax.fori_loop(0, tt // DISPATCH_UNROLL, drain, 0)


def _dispatch(h2_tiled, dest, fill_lo, fill_n, n_slots, tt):
    n = dest.shape[1]
    tiles = n // tt
    dest_t = dest.reshape(TOP_K, tiles, tt).transpose(1, 0, 2).reshape(tiles, 1, TOP_K * tt)
    grid_spec = pltpu.PrefetchScalarGridSpec(
        num_scalar_prefetch=2,
        grid=(tiles,),
        in_specs=[pl.BlockSpec((1, 1, TOP_K * tt), lambda i, lo, cnt: (i, 0, 0), memory_space=pltpu.SMEM),
                  pl.BlockSpec((tt * ROW_TILES, LANES), lambda i, lo, cnt: (i, 0))],
        out_specs=pl.BlockSpec(memory_space=pl.ANY),
        scratch_shapes=[pltpu.VMEM((ROW_TILES, LANES), F32), pltpu.SemaphoreType.DMA(()),
                        pltpu.SemaphoreType.DMA(())],
    )
    return pl.pallas_call(
        functools.partial(_dispatch_kernel, tt=tt),
        grid_spec=grid_spec,
        out_shape=jax.ShapeDtypeStruct((n_slots * ROW_TILES, LANES), F32),
        compiler_params=_params(("arbitrary",)),
        name="dispatch",
    )(fill_lo, fill_n, dest_t, h2_tiled)


def _expert_kernel(be_ref, used_ref, xs_ref, w1_ref, b1_ref, w2_ref, b2_ref, y_ref, w1b_ref, w2b_ref, *, mb):
    b = pl.program_id(0)

    @pl.when((b == 0) | (be_ref[b] != be_ref[jnp.maximum(b - 1, 0)]))
    def _():
        w1b_ref[...] = w1_ref[0].astype(BF16)
        w2b_ref[...] = w2_ref[0].astype(BF16)

    @pl.when(b < used_ref[0])
    def _():
        x = _load_row_tiled(xs_ref, mb).astype(BF16)
        u = jnp.dot(x, w1b_ref[...], preferred_element_type=F32) + b1_ref[0]
        gate = jnp.minimum(u[:, :D_EXPERT], SWIGLU_LIMIT)
        lin = jnp.clip(u[:, D_EXPERT:], -SWIGLU_LIMIT, SWIGLU_LIMIT)
        act = gate * jax.nn.sigmoid(SWIGLU_ALPHA * gate) * (lin + 1.0)
        y = jnp.dot(act.astype(BF16), w2b_ref[...], preferred_element_type=F32) + b2_ref[0]
        _store_row_tiled(y_ref, y)

    @pl.when(b >= used_ref[0])
    def _():
        y_ref[...] = jnp.zeros_like(y_ref)


def _experts(xs, block_expert, n_used, w1, b1, w2, b2, mb):
    n_blocks = block_expert.shape[0]
    b1r = b1.reshape(N_EXPERTS, 1, 2 * D_EXPERT).astype(F32)
    b2r = b2.reshape(N_EXPERTS, 1, D_MODEL).astype(F32)
    grid_spec = pltpu.PrefetchScalarGridSpec(
        num_scalar_prefetch=2,
        grid=(n_blocks,),
        in_specs=[
            pl.BlockSpec((mb * ROW_TILES, LANES), lambda i, be, used: (jnp.minimum(i, used[0] - 1), 0)),
            pl.BlockSpec((1, D_MODEL, 2 * D_EXPERT), lambda i, be, used: (be[i], 0, 0)),
            pl.BlockSpec((1, 1, 2 * D_EXPERT), lambda i, be, used: (be[i], 0, 0)),
            pl.BlockSpec((1, D_EXPERT, D_MODEL), lambda i, be, used: (be[i], 0, 0)),
            pl.BlockSpec((1, 1, D_MODEL), lambda i, be, used: (be[i], 0, 0)),
        ],
        out_specs=pl.BlockSpec((mb * ROW_TILES, LANES), lambda i, be, used: (i, 0)),
        scratch_shapes=[pltpu.VMEM((D_MODEL, 2 * D_EXPERT), BF16), pltpu.VMEM((D_EXPERT, D_MODEL), BF16)],
    )
    return pl.pallas_call(
        functools.partial(_expert_kernel, mb=mb),
        grid_spec=grid_spec,
        out_shape=jax.ShapeDtypeStruct(xs.shape, F32),
        compiler_params=_params(("arbitrary",), vmem=EXPERT_VMEM_LIMIT),
        name="experts",
    )(block_expert, n_used, xs, w1, b1r, w2, b2r)


COMBINE_UNROLL = 4
COMBINE_CHUNK = 64


def _combine_kernel(dest_ref, dest_next_ref, x1_ref, gate_ref, fw_ref, y_ref, o_ref, buf_ref, sems, *, tc, tiles):
    i = pl.program_id(0)
    slot = i % 2

    def row_copy(d_ref, dst_slot, t, k):
        return pltpu.make_async_copy(_row_slice(y_ref, d_ref[0, 0, k * tc + t]),
                                     _row_slice(buf_ref.at[dst_slot, k], t), sems.at[dst_slot])

    def gather(d_ref, dst_slot):
        def body(g, carry):
            for u in range(COMBINE_UNROLL):
                for k in range(TOP_K):
                    row_copy(d_ref, dst_slot, g * COMBINE_UNROLL + u, k).start(priority=k % 2)
            return carry
        lax.fori_loop(0, tc // COMBINE_UNROLL, body, 0)

    @pl.when(i == 0)
    def _():
        gather(dest_ref, 0)

    def drain(dst_slot):
        def body(g, carry):
            for u in range(COMBINE_UNROLL):
                for k in range(TOP_K):
                    row_copy(dest_ref, dst_slot, g * COMBINE_UNROLL + u, k).wait()
            return carry
        lax.fori_loop(0, tc // COMBINE_UNROLL, body, 0)

    drain(slot)

    gates = gate_ref[...]
    gates_t = jnp.concatenate([gates, jnp.zeros((SUBLANES - TOP_K, tc), F32)], axis=0).T
    pieces = (tc // COMBINE_CHUNK) * ROW_TILES
    per_piece = tc // pieces
    piece = 0
    for c in range(tc // COMBINE_CHUNK):
        tok = slice(c * COMBINE_CHUNK, (c + 1) * COMBINE_CHUNK)
        gate_cols = [jnp.broadcast_to(gates_t[tok, k:k + 1], (COMBINE_CHUNK, LANES)) for k in range(TOP_K)]
        sum_sq = jnp.zeros((COMBINE_CHUNK, 1), F32)
        for s in range(ROW_TILES):
            slab = slice(s * LANES, (s + 1) * LANES)
            xo = x1_ref[tok, slab]
            for k in range(TOP_K):
                rows = pl.ds(c * COMBINE_CHUNK * ROW_TILES + s, COMBINE_CHUNK, stride=ROW_TILES)
                xo = xo + gate_cols[k] * buf_ref[slot, k, rows, :]
            o_ref[tok, slab] = xo
            sum_sq = sum_sq + jnp.sum(xo * xo, axis=-1, keepdims=True)
            for t in range(piece * per_piece, (piece + 1) * per_piece):
                for k in range(TOP_K):
                    row_copy(dest_next_ref, 1 - slot, t, k).start(priority=k % 2)
            piece += 1
        inv = lax.rsqrt(sum_sq * (1.0 / D_MODEL) + RMS_EPS)
        o_ref[tok, :] = o_ref[tok, :] * inv * fw_ref[...]

    @pl.when(i == tiles - 1)
    def _():
        drain(1 - slot)


def _combine(x1, gates, dest, y, final_w, tc):
    n = x1.shape[0]
    tiles = n // tc
    dest_t = dest.reshape(TOP_K, tiles, tc).transpose(1, 0, 2).reshape(tiles, 1, TOP_K * tc)
    fw = final_w.reshape(1, D_MODEL)
    dest_spec = lambda step: pl.BlockSpec((1, 1, TOP_K * tc), lambda i: (jnp.minimum(i + step, tiles - 1), 0, 0),
                                          memory_space=pltpu.SMEM)
    return pl.pallas_call(
        functools.partial(_combine_kernel, tc=tc, tiles=tiles),
        grid=(tiles,),
        in_specs=[dest_spec(0), dest_spec(1),
                  pl.BlockSpec((tc, D_MODEL), lambda i: (i, 0)),
                  pl.BlockSpec((TOP_K, tc), lambda i: (0, i)),
                  pl.BlockSpec((1, D_MODEL), lambda i: (0, 0)),
                  pl.BlockSpec(memory_space=pl.ANY)],
        out_specs=pl.BlockSpec((tc, D_MODEL), lambda i: (i, 0)),
        out_shape=jax.ShapeDtypeStruct((n, D_MODEL), F32),
        scratch_shapes=[pltpu.VMEM((2, TOP_K, tc * ROW_TILES, LANES), F32), pltpu.SemaphoreType.DMA((2,))],
        compiler_params=_params(("arbitrary",)),
        name="combine",
    )(dest_t, dest_t, x1, gates, fw, y)


PROJ_ROWS = 512
EXPERT_ROWS = 512
DISPATCH_ROWS = 512
COMBINE_ROWS = 256


def _slot_layout(counts, n_pairs, mb):
    padded = (counts + mb - 1) // mb * mb
    pad_end = jnp.cumsum(padded)
    pad_start = pad_end - padded
    n_blocks = n_pairs // mb + N_EXPERTS
    block_row0 = jnp.arange(n_blocks, dtype=jnp.int32) * mb
    block_expert = jnp.minimum(
        jnp.sum((pad_end[None, :] <= block_row0[:, None]).astype(jnp.int32), axis=1), N_EXPERTS - 1)
    n_used = (pad_end[-1:] // mb).astype(jnp.int32)
    return pad_start.astype(jnp.int32), (pad_start + counts).astype(jnp.int32), (padded - counts).astype(jnp.int32), \
        block_expert, n_used, n_blocks


def _layer(x2, batch, seq, norm1_w, w_in, ret_norm_w, w_out, norm2_w, w_router, b_router, w1, b1, w2, b2,
           final_w):
    n = x2.shape[0]
    tm = min(PROJ_ROWS, seq)
    rq, rk, rv, rg, mq, mk, mvt = _in_projection(x2, norm1_w, w_in, batch, seq, tm)
    ret = _retention(rq, rk, rv, rg, ret_norm_w, batch, seq)
    moba = _moba(mq, mk, mvt, batch, seq)
    x1, h2_tiled, top_e, gates, rank, cnt = _route(x2, ret, moba, w_out, norm2_w, w_router, b_router, tm)
    counts = cnt[:, 0].astype(jnp.int32)
    mb = EXPERT_ROWS
    pad_start, fill_lo, fill_n, block_expert, n_used, n_blocks = _slot_layout(counts, n * TOP_K, mb)
    expert_ids = jnp.arange(N_EXPERTS, dtype=jnp.int32)[:, None, None]
    dest = rank + jnp.sum(jnp.where(top_e[None] == expert_ids, pad_start[:, None, None], 0), axis=0)
    xs = _dispatch(h2_tiled, dest, fill_lo, fill_n, n_blocks * mb, min(DISPATCH_ROWS, n))
    y = _experts(xs, block_expert, n_used, w1, b1, w2, b2, mb)
    return _combine(x1, gates, dest, y, final_w, min(COMBINE_ROWS, n))


def kernel(x, norm1_w, w_in, ret_norm_w, w_out, norm2_w, w_router, b_router, w1, b1, w2, b2, final_norm_w):
    batch, seq, d = x.shape
    depth = w_in.shape[0]
    assert depth == 1 and d == D_MODEL and seq % MOBA_BLOCK == 0
    x2 = x.reshape(batch * seq, d)
    out = _layer(x2, batch, seq, norm1_w[0], w_in[0], ret_norm_w[0], w_out[0], norm2_w[0], w_router[0],
                 b_router[0], w1[0], b1[0], w2[0], b2[0], final_norm_w)
    return out.reshape(batch, seq, d)
```

```python
import functools

import numpy as np
import jax
import jax.numpy as jnp
from jax import lax
from jax.experimental import pallas as pl
from jax.experimental.pallas import tpu as pltpu

D_MODEL = 1024
RET_HEADS = 4
RET_DK = 64
RET_DV = 128
RET_CHUNK = 128
RET_THETA = 10000.0
MOBA_HEADS = 8
MOBA_DH = 64
MOBA_BLOCK = 256
MOBA_TOPK = 3
ROPE_THETA = 500000.0
ROPE_DIM = MOBA_DH // 4
RET_QK_W = RET_HEADS * RET_DK
RET_V_W = RET_HEADS * RET_DV
MOBA_W = MOBA_HEADS * MOBA_DH
N_EXPERTS = 32
TOP_K = 4
D_EXPERT = D_MODEL
SWIGLU_LIMIT = 7.0
SWIGLU_ALPHA = 1.702
RMS_EPS = 1e-5
GN_EPS = 1e-6
NEG_INF = -1e30

SUBLANES = 8
LANES = 128
ROW_TILES = D_MODEL // LANES
VMEM_LIMIT = 48 * 1024 * 1024
EXPERT_VMEM_LIMIT = 56 * 1024 * 1024

F32 = jnp.float32
BF16 = jnp.bfloat16
NT_DIMS = (((1,), (1,)), ((), ()))
TN_DIMS = (((0,), (0,)), ((), ()))


def _params(sem, vmem=VMEM_LIMIT, flags=None):
    return pltpu.CompilerParams(dimension_semantics=sem, vmem_limit_bytes=vmem, flags=flags)


def _rotate(x, cos, sin_lo, sin_hi, half):
    width = x.shape[-1]
    nxt = pltpu.roll(x, width - half, axis=1)
    prv = pltpu.roll(x, half, axis=1)
    return x * cos + nxt * sin_lo + prv * sin_hi


def _inproj_kernel(x_ref, nw_ref, w_ref, wvt_ref, rc_ref, rlo_ref, rhi_ref, mc_ref, mlo_ref, mhi_ref,
                   rq_ref, rk_ref, rv_ref, rg_ref, mq_ref, mk_ref, mvt_ref):
    x = x_ref[...]
    ms = jnp.mean(x * x, axis=-1, keepdims=True)
    h = (x * lax.rsqrt(ms + RMS_EPS) * nw_ref[...]).astype(BF16)
    proj = jnp.dot(h, w_ref[...], preferred_element_type=F32)
    o = 0
    rq = proj[:, o:o + RET_QK_W]; o += RET_QK_W
    rk = proj[:, o:o + RET_QK_W]; o += RET_QK_W
    rv = proj[:, o:o + RET_V_W]; o += RET_V_W
    rg = proj[:, o:o + RET_V_W]; o += RET_V_W
    mq = proj[:, o:o + MOBA_W]; o += MOBA_W
    mk = proj[:, o:o + MOBA_W]
    rc, rlo, rhi = rc_ref[...], rlo_ref[...], rhi_ref[...]
    mc, mlo, mhi = mc_ref[...], mlo_ref[...], mhi_ref[...]
    rq_ref[...] = _rotate(rq, rc, rlo, rhi, RET_DK // 2).astype(BF16)
    rk_ref[...] = _rotate(rk, rc, rlo, rhi, RET_DK // 2).astype(BF16)
    rv_ref[...] = rv.astype(BF16)
    rg_ref[...] = rg.astype(BF16)
    mq_ref[...] = (_rotate(mq, mc, mlo, mhi, ROPE_DIM // 2) * MOBA_Q_SCALE).astype(BF16)
    mk_ref[...] = _rotate(mk, mc, mlo, mhi, ROPE_DIM // 2).astype(BF16)
    mvt_ref[0] = lax.dot_general(wvt_ref[...], h, NT_DIMS, preferred_element_type=F32).astype(BF16)


def _rotary_tables(seq, rot_dim, theta, head_dim, heads):
    half = rot_dim // 2
    inv_freq = (np.float32(theta) ** (np.float32(-2.0) * np.arange(half, dtype=np.float32) / np.float32(rot_dim)))
    ang = np.arange(seq, dtype=np.float32)[:, None] * inv_freq.astype(np.float32)[None, :]
    cos, sin = np.cos(ang).astype(np.float32), np.sin(ang).astype(np.float32)
    rest = head_dim - rot_dim
    ones = np.ones((seq, rest), np.float32)
    zeros = np.zeros((seq, rest), np.float32)
    zh = np.zeros((seq, half), np.float32)
    cos_t = np.concatenate([cos, cos, ones], axis=-1)
    lo_t = np.concatenate([-sin, zh, zeros], axis=-1)
    hi_t = np.concatenate([zh, sin, zeros], axis=-1)
    tile = lambda t: jnp.asarray(np.tile(t, (1, heads)))
    return tile(cos_t), tile(lo_t), tile(hi_t)


def _in_projection(x2, norm_w, w_in, batch, seq, tm):
    n = x2.shape[0]
    st = seq // tm
    w_main = w_in[:, :D_MODEL * 3 - MOBA_W].astype(BF16)
    w_vt = w_in[:, D_MODEL * 3 - MOBA_W:].T.astype(BF16)
    wm = w_main.shape[1]
    r_tabs = _rotary_tables(seq, RET_DK, RET_THETA, RET_DK, RET_HEADS)
    m_tabs = _rotary_tables(seq, ROPE_DIM, ROPE_THETA, MOBA_DH, MOBA_HEADS)
    row = lambda w: pl.BlockSpec((tm, w), lambda b, s: (b * st + s, 0))
    pos = lambda w: pl.BlockSpec((tm, w), lambda b, s: (s, 0))
    full = lambda a: pl.BlockSpec(a.shape, lambda b, s: (0,) * a.ndim)
    nw = norm_w.reshape(1, D_MODEL)
    out_shape = (
        jax.ShapeDtypeStruct((n, RET_QK_W), BF16), jax.ShapeDtypeStruct((n, RET_QK_W), BF16),
        jax.ShapeDtypeStruct((n, RET_V_W), BF16), jax.ShapeDtypeStruct((n, RET_V_W), BF16),
        jax.ShapeDtypeStruct((n, MOBA_W), BF16), jax.ShapeDtypeStruct((n, MOBA_W), BF16),
        jax.ShapeDtypeStruct((batch, MOBA_W, seq), BF16),
    )
    return pl.pallas_call(
        _inproj_kernel,
        grid=(batch, st),
        in_specs=[row(D_MODEL), full(nw), full(w_main), full(w_vt),
                  pos(RET_QK_W), pos(RET_QK_W), pos(RET_QK_W), pos(MOBA_W), pos(MOBA_W), pos(MOBA_W)],
        out_specs=(row(RET_QK_W), row(RET_QK_W), row(RET_V_W), row(RET_V_W), row(MOBA_W), row(MOBA_W),
                   pl.BlockSpec((1, MOBA_W, tm), lambda b, s: (b, 0, s))),
        out_shape=out_shape,
        compiler_params=_params(("parallel", "parallel")),
        name="in_projection",
    )(x2, nw, w_main, w_vt, *r_tabs, *m_tabs)


RET_UNROLL = 4


def _retention_kernel(q_ref, k_ref, v_ref, g_ref, decay_ref, xi_ref, zeta_ref, gnw_ref, o_ref, state_ref,
                      *, n_chunks, chunk_decay):
    c = RET_CHUNK
    state_ref[...] = jnp.zeros_like(state_ref)

    def chunk_group(gi, carry):
        states = [state_ref[h] for h in range(RET_HEADS)]
        for u in range(RET_UNROLL):
            r0 = pl.multiple_of((gi * RET_UNROLL + u) * c, c)
            q = q_ref[pl.ds(r0, c), :]
            k = k_ref[pl.ds(r0, c), :]
            v = v_ref[pl.ds(r0, c), :]
            g = g_ref[pl.ds(r0, c), :].astype(F32)
            qx = (q.astype(F32) * xi_ref[...]).astype(BF16)
            kz = (k.astype(F32) * zeta_ref[...]).astype(BF16)
            outs = []
            for h in range(RET_HEADS):
                qs = slice(h * RET_DK, (h + 1) * RET_DK)
                vs = slice(h * RET_DV, (h + 1) * RET_DV)
                vh = v[:, vs]
                scores = lax.dot_general(q[:, qs], k[:, qs], NT_DIMS, preferred_element_type=F32) * decay_ref[h]
                inner = jnp.dot(scores.astype(BF16), vh, preferred_element_type=F32)
                cross = jnp.dot(qx[:, qs], states[h].astype(BF16), preferred_element_type=F32)
                states[h] = chunk_decay[h] * states[h] + lax.dot_general(
                    kz[:, qs], vh, TN_DIMS, preferred_element_type=F32)
                ret = inner + cross
                mu = jnp.mean(ret, axis=-1, keepdims=True)
                cen = ret - mu
                var = jnp.mean(cen * cen, axis=-1, keepdims=True)
                outs.append(cen * lax.rsqrt(var + GN_EPS))
            normed = jnp.concatenate(outs, axis=-1) * gnw_ref[...]
            o_ref[pl.ds(r0, c), :] = (g * jax.nn.sigmoid(g) * normed).astype(BF16)
        for h in range(RET_HEADS):
            state_ref[h] = states[h]
        return carry

    lax.fori_loop(0, n_chunks // RET_UNROLL, chunk_group, 0)


def _retention_consts():
    h = np.arange(RET_HEADS, dtype=np.float32)
    log_g = np.log1p(-np.exp2(-5.0 - h)).astype(np.float32)
    idx = np.arange(RET_CHUNK, dtype=np.float32)
    rel = idx[:, None] - idx[None, :]
    scale = np.float32(RET_DK ** -0.5)
    decay = np.where(rel >= 0, np.exp(log_g[:, None, None] * np.maximum(rel, 0.0)), 0.0).astype(np.float32) * scale
    zeta = np.exp(log_g[:, None] * (RET_CHUNK - 1.0 - idx)).astype(np.float32) * scale
    xi = np.exp(log_g[:, None] * (idx + 1.0)).astype(np.float32)
    spread = lambda t: np.repeat(t.T, RET_DK, axis=1)
    chunk_decay = tuple(float(v) for v in np.exp(log_g * RET_CHUNK).astype(np.float32))
    return decay, spread(xi), spread(zeta), chunk_decay


def _retention(rq, rk, rv, rg, gn_w, batch, seq):
    n = rq.shape[0]
    decay, xi, zeta, chunk_decay = _retention_consts()
    gnw = gn_w.reshape(1, RET_V_W).astype(F32)
    seq_blk = lambda w: pl.BlockSpec((seq, w), lambda b: (b, 0))
    full = lambda a: pl.BlockSpec(a.shape, lambda b: (0,) * a.ndim)
    kern = functools.partial(_retention_kernel, n_chunks=seq // RET_CHUNK, chunk_decay=chunk_decay)
    return pl.pallas_call(
        kern,
        grid=(batch,),
        in_specs=[seq_blk(RET_QK_W), seq_blk(RET_QK_W), seq_blk(RET_V_W), seq_blk(RET_V_W),
                  full(decay), full(xi), full(zeta), full(gnw)],
        out_specs=seq_blk(RET_V_W),
        out_shape=jax.ShapeDtypeStruct((n, RET_V_W), BF16),
        scratch_shapes=[pltpu.VMEM((RET_HEADS, RET_DK, RET_DV), F32)],
        compiler_params=_params(("parallel",)),
        name="retention",
    )(rq, rk, rv, rg, jnp.asarray(decay), jnp.asarray(xi), jnp.asarray(zeta), gnw)


HEADS_PER_STEP = LANES // MOBA_DH
MOBA_Q_SCALE = MOBA_DH ** -0.5 * float(np.log2(np.e))


def _moba_kernel(q_ref, k_ref, vt_ref, o_ref, *, n_blocks):
    blk = MOBA_BLOCK
    dh = MOBA_DH
    blk_id = lax.broadcasted_iota(jnp.int32, (n_blocks, blk), 0)
    key_pos = lax.broadcasted_iota(jnp.int32, (blk, blk), 0)
    qry_pos = lax.broadcasted_iota(jnp.int32, (blk, blk), 1)
    causal = key_pos <= qry_pos
    rows = lambda b: slice(b * blk, (b + 1) * blk)

    def masked_scores(hh, j, k_mean):
        lanes = slice(hh * dh, (hh + 1) * dh)
        q = q_ref[rows(j), lanes]
        n_keys = (j + 1) * blk
        s_all = lax.dot_general(k_ref[0:n_keys, lanes], q, NT_DIMS, preferred_element_type=F32)
        if j > 0:
            gate = lax.dot_general(k_mean, q.astype(F32), NT_DIMS, preferred_element_type=F32)
            past = blk_id < j
            gate = jnp.where(past, gate, NEG_INF)
            rank = jnp.zeros((n_blocks, blk), jnp.int32)
            for other in range(j):
                g_o = gate[other:other + 1, :]
                ahead = (g_o > gate) | ((g_o == gate) & (other < blk_id))
                rank = rank + ahead.astype(jnp.int32)
            bias = jnp.where(past & (rank < MOBA_TOPK), 0.0, NEG_INF)
        s_blocks = [s_all[rows(i), :] + bias[i:i + 1, :] for i in range(j)]
        s_blocks.append(jnp.where(causal, s_all[rows(j), :], NEG_INF))
        m = jnp.max(s_blocks[0], axis=0, keepdims=True)
        for s in s_blocks[1:]:
            m = jnp.maximum(m, jnp.max(s, axis=0, keepdims=True))
        return s_blocks, m

    def softmax_weights(s_blocks, m):
        p_blocks = [jnp.exp2(s - m) for s in s_blocks]
        l = jnp.sum(p_blocks[0], axis=0, keepdims=True)
        for p in p_blocks[1:]:
            l = l + jnp.sum(p, axis=0, keepdims=True)
        return jnp.concatenate([p.astype(BF16) for p in p_blocks], axis=0), l

    def weighted_values(hh, j, p_all, l):
        lanes = slice(hh * dh, (hh + 1) * dh)
        acc = jnp.dot(vt_ref[0, lanes, 0:(j + 1) * blk], p_all, preferred_element_type=F32)
        return (acc / l).T

    k_means = []
    for hh in range(HEADS_PER_STEP):
        kf = k_ref[:, hh * dh:(hh + 1) * dh].astype(F32)
        k_means.append(jnp.mean(kf.reshape(n_blocks, blk, dh), axis=1))
    units = [(j, hh) for j in range(n_blocks) for hh in range(HEADS_PER_STEP)]
    scored, weighted, outs = {}, {}, {}
    for step in range(len(units) + 2):
        if step < len(units):
            j, hh = units[step]
            scored[step] = masked_scores(hh, j, k_means[hh])
        if 0 <= step - 1 < len(units):
            weighted[step - 1] = softmax_weights(*scored.pop(step - 1))
        if 0 <= step - 2 < len(units):
            j, hh = units[step - 2]
            outs[(j, hh)] = weighted_values(hh, j, *weighted.pop(step - 2))
            if hh == HEADS_PER_STEP - 1:
                o_ref[rows(j), :] = jnp.concatenate(
                    [outs.pop((j, h)) for h in range(HEADS_PER_STEP)], axis=-1).astype(BF16)


def _moba(mq, mk, mvt, batch, seq):
    n = mq.shape[0]
    n_blocks = seq // MOBA_BLOCK
    pairs = MOBA_HEADS // HEADS_PER_STEP
    qk_spec = pl.BlockSpec((seq, LANES), lambda b, p: (b, p))
    kern = functools.partial(_moba_kernel, n_blocks=n_blocks)
    return pl.pallas_call(
        kern,
        grid=(batch, pairs),
        in_specs=[qk_spec, qk_spec, pl.BlockSpec((1, LANES, seq), lambda b, p: (b, p, 0))],
        out_specs=qk_spec,
        out_shape=jax.ShapeDtypeStruct((n, MOBA_W), BF16),
        compiler_params=_params(("parallel", "parallel")),
        name="moba",
    )(mq, mk, mvt)


def _store_row_tiled(ref, val):
    rows = val.shape[0]
    for s in range(ROW_TILES):
        ref[pl.ds(s, rows, stride=ROW_TILES), :] = val[:, s * LANES:(s + 1) * LANES]


def _load_row_tiled(ref, rows, lead=()):
    parts = []
    for s in range(ROW_TILES):
        parts.append(ref[tuple(lead) + (pl.ds(s, rows, stride=ROW_TILES), slice(None))])
    return jnp.concatenate(parts, axis=-1)


def _route_kernel(x_ref, ret_ref, moba_ref, wor_ref, wom_ref, nw_ref, wrt_ref, br_ref, tri_ref,
                  x1_ref, h2_ref, te_ref, gate_ref, rank_ref, cnt_ref, carry_ref):
    @pl.when(pl.program_id(0) == 0)
    def _():
        carry_ref[...] = jnp.zeros_like(carry_ref)

    x1 = (x_ref[...]
          + jnp.dot(ret_ref[...], wor_ref[...], preferred_element_type=F32)
          + jnp.dot(moba_ref[...], wom_ref[...], preferred_element_type=F32))
    x1_ref[...] = x1
    ms = jnp.mean(x1 * x1, axis=-1, keepdims=True)
    h2 = x1 * lax.rsqrt(ms + RMS_EPS) * nw_ref[...]
    _store_row_tiled(h2_ref, h2)
    logits = lax.dot_general(wrt_ref[...], h2, NT_DIMS, precision=lax.Precision.HIGHEST,
                             preferred_element_type=F32) + br_ref[...]
    tm = logits.shape[1]
    e_id = lax.broadcasted_iota(jnp.int32, (N_EXPERTS, tm), 0)
    vals, ids = [], []
    for _ in range(TOP_K):
        m = jnp.max(logits, axis=0, keepdims=True)
        idx = jnp.min(jnp.where(logits == m, e_id, N_EXPERTS), axis=0, keepdims=True)
        vals.append(m)
        ids.append(idx)
        logits = jnp.where(e_id == idx, -jnp.inf, logits)
    exps = [jnp.exp(v - vals[0]) for v in vals]
    denom = exps[0] + exps[1] + exps[2] + exps[3]
    gate_ref[...] = jnp.concatenate([e / denom for e in exps], axis=0)
    te_ref[...] = jnp.concatenate(ids, axis=0)
    hits = [e_id == idx for idx in ids]
    picked = (hits[0] | hits[1] | hits[2] | hits[3])
    picked_f = jnp.where(picked, 1.0, 0.0)
    before = jnp.dot(picked_f.astype(BF16), tri_ref[...], preferred_element_type=F32) + carry_ref[...]
    ranks = [jnp.sum(jnp.where(hit, before, 0.0), axis=0, keepdims=True) for hit in hits]
    rank_ref[...] = jnp.concatenate(ranks, axis=0).astype(jnp.int32)
    carry_ref[...] = carry_ref[...] + jnp.sum(picked_f, axis=1, keepdims=True)
    cnt_ref[...] = jnp.broadcast_to(carry_ref[...], cnt_ref.shape)


def _route(x2, ret, moba, w_out, norm_w, w_router, b_router, tm):
    n = x2.shape[0]
    wo_r = w_out[:RET_V_W].astype(BF16)
    wo_m = w_out[RET_V_W:].astype(BF16)
    nw = norm_w.reshape(1, D_MODEL)
    wrt = w_router.T.astype(F32)
    br = b_router.reshape(N_EXPERTS, 1).astype(F32)
    tri = (np.arange(tm)[:, None] < np.arange(tm)[None, :]).astype(np.float32)
    tri = jnp.asarray(tri, dtype=BF16)
    row = lambda w: pl.BlockSpec((tm, w), lambda i: (i, 0))
    col = pl.BlockSpec((TOP_K, tm), lambda i: (0, i))
    full = lambda a: pl.BlockSpec(a.shape, lambda i: (0,) * a.ndim)
    out_shape = (
        jax.ShapeDtypeStruct((n, D_MODEL), F32),
        jax.ShapeDtypeStruct((n * ROW_TILES, LANES), F32),
        jax.ShapeDtypeStruct((TOP_K, n), jnp.int32),
        jax.ShapeDtypeStruct((TOP_K, n), F32),
        jax.ShapeDtypeStruct((TOP_K, n), jnp.int32),
        jax.ShapeDtypeStruct((N_EXPERTS, LANES), F32),
    )
    return pl.pallas_call(
        _route_kernel,
        grid=(n // tm,),
        in_specs=[row(D_MODEL), row(RET_V_W), row(MOBA_W), full(wo_r), full(wo_m), full(nw), full(wrt),
                  full(br), full(tri)],
        out_specs=(row(D_MODEL), pl.BlockSpec((tm * ROW_TILES, LANES), lambda i: (i, 0)), col, col, col,
                   pl.BlockSpec((N_EXPERTS, LANES), lambda i: (0, 0))),
        out_shape=out_shape,
        scratch_shapes=[pltpu.VMEM((N_EXPERTS, 1), F32)],
        compiler_params=_params(("arbitrary",)),
        name="out_proj_route",
    )(x2, ret, moba, wo_r, wo_m, nw, wrt, br, tri)


def _row_slice(ref, row):
    return ref.at[pl.ds(pl.multiple_of(row * ROW_TILES, ROW_TILES), ROW_TILES)]


DISPATCH_UNROLL = 4


def _dispatch_kernel(fill_lo_ref, fill_n_ref, dest_ref, h2_ref, xs_ref, zero_ref, sem, fill_sem, *, tt):
    @pl.when(pl.program_id(0) == 0)
    def _():
        zero_ref[...] = jnp.zeros_like(zero_ref)

        def fill_copy(e, r):
            return pltpu.make_async_copy(zero_ref, _row_slice(xs_ref, fill_lo_ref[e] + r), fill_sem)

        def fill(e, carry):
            def one(r, c):
                fill_copy(e, r).start()
                return c
            return lax.fori_loop(0, fill_n_ref[e], one, carry)

        def fill_wait(e, carry):
            def one(r, c):
                fill_copy(e, r).wait()
                return c
            return lax.fori_loop(0, fill_n_ref[e], one, carry)

        lax.fori_loop(0, N_EXPERTS, fill, 0)
        lax.fori_loop(0, N_EXPERTS, fill_wait, 0)

    def row_copy(t, k):
        return pltpu.make_async_copy(_row_slice(h2_ref, t), _row_slice(xs_ref, dest_ref[0, 0, k * tt + t]), sem)

    def issue(g, carry):
        for u in range(DISPATCH_UNROLL):
            for k in range(TOP_K):
                row_copy(g * DISPATCH_UNROLL + u, k).start(priority=k % 2)
        return carry

    lax.fori_loop(0, tt // DISPATCH_UNROLL, issue, 0)

    def drain(g, carry):
        for u in range(DISPATCH_UNROLL):
            for k in range(TOP_K):
                row_copy(g * DISPATCH_UNROLL + u, k).wait()
        return carry

    lax.fori_loop(0, tt // DISPATCH_UNROLL, drain, 0)


def _dispatch(h2_tiled, dest, fill_lo, fill_n, n_slots, tt):
    n = dest.shape[1]
    tiles = n // tt
    dest_t = dest.reshape(TOP_K, tiles, tt).transpose(1, 0, 2).reshape(tiles, 1, TOP_K * tt)
    grid_spec = pltpu.PrefetchScalarGridSpec(
        num_scalar_prefetch=2,
        grid=(tiles,),
        in_specs=[pl.BlockSpec((1, 1, TOP_K * tt), lambda i, lo, cnt: (i, 0, 0), memory_space=pltpu.SMEM),
                  pl.BlockSpec((tt * ROW_TILES, LANES), lambda i, lo, cnt: (i, 0))],
        out_specs=pl.BlockSpec(memory_space=pl.ANY),
        scratch_shapes=[pltpu.VMEM((ROW_TILES, LANES), F32), pltpu.SemaphoreType.DMA(()),
                        pltpu.SemaphoreType.DMA(())],
    )
    return pl.pallas_call(
        functools.partial(_dispatch_kernel, tt=tt),
        grid_spec=grid_spec,
        out_shape=jax.ShapeDtypeStruct((n_slots * ROW_TILES, LANES), F32),
        compiler_params=_params(("arbitrary",)),
        name="dispatch",
    )(fill_lo, fill_n, dest_t, h2_tiled)


def _expert_kernel(be_ref, used_ref, xs_ref, w1_ref, b1_ref, w2_ref, b2_ref, y_ref, w1b_ref, w2b_ref, *, mb):
    b = pl.program_id(0)

    @pl.when((b == 0) | (be_ref[b] != be_ref[jnp.maximum(b - 1, 0)]))
    def _():
        w1b_ref[...] = w1_ref[0].astype(BF16)
        w2b_ref[...] = w2_ref[0].astype(BF16)

    @pl.when(b < used_ref[0])
    def _():
        x = _load_row_tiled(xs_ref, mb).astype(BF16)
        u = jnp.dot(x, w1b_ref[...], preferred_element_type=F32) + b1_ref[0]
        gate = jnp.minimum(u[:, :D_EXPERT], SWIGLU_LIMIT)
        lin = jnp.clip(u[:, D_EXPERT:], -SWIGLU_LIMIT, SWIGLU_LIMIT)
        act = gate * jax.nn.sigmoid(SWIGLU_ALPHA * gate) * (lin + 1.0)
        y = jnp.dot(act.astype(BF16), w2b_ref[...], preferred_element_type=F32) + b2_ref[0]
        _store_row_tiled(y_ref, y)

    @pl.when(b >= used_ref[0])
    def _():
        y_ref[...] = jnp.zeros_like(y_ref)


def _experts(xs, block_expert, n_used, w1, b1, w2, b2, mb):
    n_blocks = block_expert.shape[0]
    b1r = b1.reshape(N_EXPERTS, 1, 2 * D_EXPERT).astype(F32)
    b2r = b2.reshape(N_EXPERTS, 1, D_MODEL).astype(F32)
    grid_spec = pltpu.PrefetchScalarGridSpec(
        num_scalar_prefetch=2,
        grid=(n_blocks,),
        in_specs=[
            pl.BlockSpec((mb * ROW_TILES, LANES), lambda i, be, used: (jnp.minimum(i, used[0] - 1), 0)),
            pl.BlockSpec((1, D_MODEL, 2 * D_EXPERT), lambda i, be, used: (be[i], 0, 0)),
            pl.BlockSpec((1, 1, 2 * D_EXPERT), lambda i, be, used: (be[i], 0, 0)),
            pl.BlockSpec((1, D_EXPERT, D_MODEL), lambda i, be, used: (be[i], 0, 0)),
            pl.BlockSpec((1, 1, D_MODEL), lambda i, be, used: (be[i], 0, 0)),
        ],
        out_specs=pl.BlockSpec((mb * ROW_TILES, LANES), lambda i, be, used: (i, 0)),
        scratch_shapes=[pltpu.VMEM((D_MODEL, 2 * D_EXPERT), BF16), pltpu.VMEM((D_EXPERT, D_MODEL), BF16)],
    )
    return pl.pallas_call(
        functools.partial(_expert_kernel, mb=mb),
        grid_spec=grid_spec,
        out_shape=jax.ShapeDtypeStruct(xs.shape, F32),
        compiler_params=_params(("arbitrary",), vmem=EXPERT_VMEM_LIMIT),
        name="experts",
    )(block_expert, n_used, xs, w1, b1r, w2, b2r)


COMBINE_UNROLL = 4
COMBINE_CHUNK = 64


def _combine_kernel(dest_ref, dest_next_ref, x1_ref, gate_ref, fw_ref, y_ref, o_ref, buf_ref, sems, *, tc, tiles):
    i = pl.program_id(0)
    slot = i % 2

    def row_copy(d_ref, dst_slot, t, k):
        return pltpu.make_async_copy(_row_slice(y_ref, d_ref[0, 0, k * tc + t]),
                                     _row_slice(buf_ref.at[dst_slot, k], t), sems.at[dst_slot])

    def gather(d_ref, dst_slot):
        def body(g, carry):
            for u in range(COMBINE_UNROLL):
                for k in range(TOP_K):
                    row_copy(d_ref, dst_slot, g * COMBINE_UNROLL + u, k).start(priority=k % 2)
            return carry
        lax.fori_loop(0, tc // COMBINE_UNROLL, body, 0)

    @pl.when(i == 0)
    def _():
        gather(dest_ref, 0)

    def drain(dst_slot):
        def body(g, carry):
            for u in range(COMBINE_UNROLL):
                for k in range(TOP_K):
                    row_copy(dest_ref, dst_slot, g * COMBINE_UNROLL + u, k).wait()
            return carry
        lax.fori_loop(0, tc // COMBINE_UNROLL, body, 0)

    drain(slot)

    gates = gate_ref[...]
    gates_t = jnp.concatenate([gates, jnp.zeros((SUBLANES - TOP_K, tc), F32)], axis=0).T
    pieces = (tc // COMBINE_CHUNK) * ROW_TILES
    per_piece = tc // pieces
    piece = 0
    for c in range(tc // COMBINE_CHUNK):
        tok = slice(c * COMBINE_CHUNK, (c + 1) * COMBINE_CHUNK)
        gate_cols = [jnp.broadcast_to(gates_t[tok, k:k + 1], (COMBINE_CHUNK, LANES)) for k in range(TOP_K)]
        sum_sq = jnp.zeros((COMBINE_CHUNK, 1), F32)
        for s in range(ROW_TILES):
            slab = slice(s * LANES, (s + 1) * LANES)
            xo = x1_ref[tok, slab]
            for k in range(TOP_K):
                rows = pl.ds(c * COMBINE_CHUNK * ROW_TILES + s, COMBINE_CHUNK, stride=ROW_TILES)
                xo = xo + gate_cols[k] * buf_ref[slot, k, rows, :]
            o_ref[tok, slab] = xo
            sum_sq = sum_sq + jnp.sum(xo * xo, axis=-1, keepdims=True)
            for t in range(piece * per_piece, (piece + 1) * per_piece):
                for k in range(TOP_K):
                    row_copy(dest_next_ref, 1 - slot, t, k).start(priority=k % 2)
            piece += 1
        inv = lax.rsqrt(sum_sq * (1.0 / D_MODEL) + RMS_EPS)
        o_ref[tok, :] = o_ref[tok, :] * inv * fw_ref[...]

    @pl.when(i == tiles - 1)
    def _():
        drain(1 - slot)


def _combine(x1, gates, dest, y, final_w, tc):
    n = x1.shape[0]
    tiles = n // tc
    dest_t = dest.reshape(TOP_K, tiles, tc).transpose(1, 0, 2).reshape(tiles, 1, TOP_K * tc)
    fw = final_w.reshape(1, D_MODEL)
    dest_spec = lambda step: pl.BlockSpec((1, 1, TOP_K * tc), lambda i: (jnp.minimum(i + step, tiles - 1), 0, 0),
                                          memory_space=pltpu.SMEM)
    return pl.pallas_call(
        functools.partial(_combine_kernel, tc=tc, tiles=tiles),
        grid=(tiles,),
        in_specs=[dest_spec(0), dest_spec(1),
                  pl.BlockSpec((tc, D_MODEL), lambda i: (i, 0)),
                  pl.BlockSpec((TOP_K, tc), lambda i: (0, i)),
                  pl.BlockSpec((1, D_MODEL), lambda i: (0, 0)),
                  pl.BlockSpec(memory_space=pl.ANY)],
        out_specs=pl.BlockSpec((tc, D_MODEL), lambda i: (i, 0)),
        out_shape=jax.ShapeDtypeStruct((n, D_MODEL), F32),
        scratch_shapes=[pltpu.VMEM((2, TOP_K, tc * ROW_TILES, LANES), F32), pltpu.SemaphoreType.DMA((2,))],
        compiler_params=_params(("arbitrary",)),
        name="combine",
    )(dest_t, dest_t, x1, gates, fw, y)


PROJ_ROWS = 512
EXPERT_ROWS = 512
DISPATCH_ROWS = 2048
COMBINE_ROWS = 256


def _slot_layout(counts, n_pairs, mb):
    padded = (counts + mb - 1) // mb * mb
    pad_end = jnp.cumsum(padded)
    pad_start = pad_end - padded
    n_blocks = n_pairs // mb + N_EXPERTS
    block_row0 = jnp.arange(n_blocks, dtype=jnp.int32) * mb
    block_expert = jnp.minimum(
        jnp.sum((pad_end[None, :] <= block_row0[:, None]).astype(jnp.int32), axis=1), N_EXPERTS - 1)
    n_used = (pad_end[-1:] // mb).astype(jnp.int32)
    return pad_start.astype(jnp.int32), (pad_start + counts).astype(jnp.int32), (padded - counts).astype(jnp.int32), \
        block_expert, n_used, n_blocks


def _layer(x2, batch, seq, norm1_w, w_in, ret_norm_w, w_out, norm2_w, w_router, b_router, w1, b1, w2, b2,
           final_w):
    n = x2.shape[0]
    tm = min(PROJ_ROWS, seq)
    rq, rk, rv, rg, mq, mk, mvt = _in_projection(x2, norm1_w, w_in, batch, seq, tm)
    ret = _retention(rq, rk, rv, rg, ret_norm_w, batch, seq)
    moba = _moba(mq, mk, mvt, batch, seq)
    x1, h2_tiled, top_e, gates, rank, cnt = _route(x2, ret, moba, w_out, norm2_w, w_router, b_router, tm)
    counts = cnt[:, 0].astype(jnp.int32)
    mb = EXPERT_ROWS
    pad_start, fill_lo, fill_n, block_expert, n_used, n_blocks = _slot_layout(counts, n * TOP_K, mb)
    expert_ids = jnp.arange(N_EXPERTS, dtype=jnp.int32)[:, None, None]
    dest = rank + jnp.sum(jnp.where(top_e[None] == expert_ids, pad_start[:, None, None], 0), axis=0)
    xs = _dispatch(h2_tiled, dest, fill_lo, fill_n, n_blocks * mb, min(DISPATCH_ROWS, n))
    y = _experts(xs, block_expert, n_used, w1, b1, w2, b2, mb)
    return _combine(x1, gates, dest, y, final_w, min(COMBINE_ROWS, n))


def kernel(x, norm1_w, w_in, ret_norm_w, w_out, norm2_w, w_router, b_router, w1, b1, w2, b2, final_norm_w):
    batch, seq, d = x.shape
    depth = w_in.shape[0]
    assert depth == 1 and d == D_MODEL and seq % MOBA_BLOCK == 0
    x2 = x.reshape(batch * seq, d)
    out = _layer(x2, batch, seq, norm1_w[0], w_in[0], ret_norm_w[0], w_out[0], norm2_w[0], w_router[0],
                 b_router[0], w1[0], b1[0], w2[0], b2[0], final_norm_w)
    return out.reshape(batch, seq, d)
```

```python
import functools

import numpy as np
import jax
import jax.numpy as jnp
from jax import lax
from jax.experimental import pallas as pl
from jax.experimental.pallas import tpu as pltpu

D_MODEL = 1024
RET_HEADS = 4
RET_DK = 64
RET_DV = 128
RET_CHUNK = 128
RET_THETA = 10000.0
MOBA_HEADS = 8
MOBA_DH = 64
MOBA_BLOCK = 256
MOBA_TOPK = 3
ROPE_THETA = 500000.0
ROPE_DIM = MOBA_DH // 4
RET_QK_W = RET_HEADS * RET_DK
RET_V_W = RET_HEADS * RET_DV
MOBA_W = MOBA_HEADS * MOBA_DH
N_EXPERTS = 32
TOP_K = 4
D_EXPERT = D_MODEL
SWIGLU_LIMIT = 7.0
SWIGLU_ALPHA = 1.702
RMS_EPS = 1e-5
GN_EPS = 1e-6
NEG_INF = -1e30

SUBLANES = 8
LANES = 128
ROW_TILES = D_MODEL // LANES
VMEM_LIMIT = 48 * 1024 * 1024
EXPERT_VMEM_LIMIT = 56 * 1024 * 1024

F32 = jnp.float32
BF16 = jnp.bfloat16
NT_DIMS = (((1,), (1,)), ((), ()))
TN_DIMS = (((0,), (0,)), ((), ()))


def _params(sem, vmem=VMEM_LIMIT, flags=None):
    return pltpu.CompilerParams(dimension_semantics=sem, vmem_limit_bytes=vmem, flags=flags)


def _rotate(x, cos, sin_lo, sin_hi, half):
    width = x.shape[-1]
    nxt = pltpu.roll(x, width - half, axis=1)
    prv = pltpu.roll(x, half, axis=1)
    return x * cos + nxt * sin_lo + prv * sin_hi


def _inproj_kernel(x_ref, nw_ref, w_ref, wvt_ref, rc_ref, rlo_ref, rhi_ref, mc_ref, mlo_ref, mhi_ref,
                   rq_ref, rk_ref, rv_ref, rg_ref, mq_ref, mk_ref, mvt_ref):
    x = x_ref[...]
    ms = jnp.mean(x * x, axis=-1, keepdims=True)
    h = (x * lax.rsqrt(ms + RMS_EPS) * nw_ref[...]).astype(BF16)
    proj = jnp.dot(h, w_ref[...], preferred_element_type=F32)
    o = 0
    rq = proj[:, o:o + RET_QK_W]; o += RET_QK_W
    rk = proj[:, o:o + RET_QK_W]; o += RET_QK_W
    rv = proj[:, o:o + RET_V_W]; o += RET_V_W
    rg = proj[:, o:o + RET_V_W]; o += RET_V_W
    mq = proj[:, o:o + MOBA_W]; o += MOBA_W
    mk = proj[:, o:o + MOBA_W]
    rc, rlo, rhi = rc_ref[...], rlo_ref[...], rhi_ref[...]
    mc, mlo, mhi = mc_ref[...], mlo_ref[...], mhi_ref[...]
    rq_ref[...] = _rotate(rq, rc, rlo, rhi, RET_DK // 2).astype(BF16)
    rk_ref[...] = _rotate(rk, rc, rlo, rhi, RET_DK // 2).astype(BF16)
    rv_ref[...] = rv.astype(BF16)
    rg_ref[...] = rg.astype(BF16)
    mq_ref[...] = (_rotate(mq, mc, mlo, mhi, ROPE_DIM // 2) * MOBA_Q_SCALE).astype(BF16)
    mk_ref[...] = _rotate(mk, mc, mlo, mhi, ROPE_DIM // 2).astype(BF16)
    mvt_ref[0] = lax.dot_general(wvt_ref[...], h, NT_DIMS, preferred_element_type=F32).astype(BF16)


def _rotary_tables(seq, rot_dim, theta, head_dim, heads):
    half = rot_dim // 2
    inv_freq = (np.float32(theta) ** (np.float32(-2.0) * np.arange(half, dtype=np.float32) / np.float32(rot_dim)))
    ang = np.arange(seq, dtype=np.float32)[:, None] * inv_freq.astype(np.float32)[None, :]
    cos, sin = np.cos(ang).astype(np.float32), np.sin(ang).astype(np.float32)
    rest = head_dim - rot_dim
    ones = np.ones((seq, rest), np.float32)
    zeros = np.zeros((seq, rest), np.float32)
    zh = np.zeros((seq, half), np.float32)
    cos_t = np.concatenate([cos, cos, ones], axis=-1)
    lo_t = np.concatenate([-sin, zh, zeros], axis=-1)
    hi_t = np.concatenate([zh, sin, zeros], axis=-1)
    tile = lambda t: jnp.asarray(np.tile(t, (1, heads)))
    return tile(cos_t), tile(lo_t), tile(hi_t)


def _in_projection(x2, norm_w, w_in, batch, seq, tm):
    n = x2.shape[0]
    st = seq // tm
    w_main = w_in[:, :D_MODEL * 3 - MOBA_W].astype(BF16)
    w_vt = w_in[:, D_MODEL * 3 - MOBA_W:].T.astype(BF16)
    wm = w_main.shape[1]
    r_tabs = _rotary_tables(seq, RET_DK, RET_THETA, RET_DK, RET_HEADS)
    m_tabs = _rotary_tables(seq, ROPE_DIM, ROPE_THETA, MOBA_DH, MOBA_HEADS)
    row = lambda w: pl.BlockSpec((tm, w), lambda b, s: (b * st + s, 0))
    pos = lambda w: pl.BlockSpec((tm, w), lambda b, s: (s, 0))
    full = lambda a: pl.BlockSpec(a.shape, lambda b, s: (0,) * a.ndim)
    nw = norm_w.reshape(1, D_MODEL)
    out_shape = (
        jax.ShapeDtypeStruct((n, RET_QK_W), BF16), jax.ShapeDtypeStruct((n, RET_QK_W), BF16),
        jax.ShapeDtypeStruct((n, RET_V_W), BF16), jax.ShapeDtypeStruct((n, RET_V_W), BF16),
        jax.ShapeDtypeStruct((n, MOBA_W), BF16), jax.ShapeDtypeStruct((n, MOBA_W), BF16),
        jax.ShapeDtypeStruct((batch, MOBA_W, seq), BF16),
    )
    return pl.pallas_call(
        _inproj_kernel,
        grid=(batch, st),
        in_specs=[row(D_MODEL), full(nw), full(w_main), full(w_vt),
                  pos(RET_QK_W), pos(RET_QK_W), pos(RET_QK_W), pos(MOBA_W), pos(MOBA_W), pos(MOBA_W)],
        out_specs=(row(RET_QK_W), row(RET_QK_W), row(RET_V_W), row(RET_V_W), row(MOBA_W), row(MOBA_W),
                   pl.BlockSpec((1, MOBA_W, tm), lambda b, s: (b, 0, s))),
        out_shape=out_shape,
        compiler_params=_params(("parallel", "parallel")),
        name="in_projection",
    )(x2, nw, w_main, w_vt, *r_tabs, *m_tabs)


RET_UNROLL = 4


def _retention_kernel(q_ref, k_ref, v_ref, g_ref, decay_ref, xi_ref, zeta_ref, gnw_ref, o_ref, state_ref,
                      *, n_chunks, chunk_decay):
    c = RET_CHUNK
    state_ref[...] = jnp.zeros_like(state_ref)

    def chunk_group(gi, carry):
        states = [state_ref[h] for h in range(RET_HEADS)]
        for u in range(RET_UNROLL):
            r0 = pl.multiple_of((gi * RET_UNROLL + u) * c, c)
            q = q_ref[pl.ds(r0, c), :]
            k = k_ref[pl.ds(r0, c), :]
            v = v_ref[pl.ds(r0, c), :]
            g = g_ref[pl.ds(r0, c), :].astype(F32)
            qx = (q.astype(F32) * xi_ref[...]).astype(BF16)
            kz = (k.astype(F32) * zeta_ref[...]).astype(BF16)
            outs = []
            for h in range(RET_HEADS):
                qs = slice(h * RET_DK, (h + 1) * RET_DK)
                vs = slice(h * RET_DV, (h + 1) * RET_DV)
                vh = v[:, vs]
                scores = lax.dot_general(q[:, qs], k[:, qs], NT_DIMS, preferred_element_type=F32) * decay_ref[h]
                inner = jnp.dot(scores.astype(BF16), vh, preferred_element_type=F32)
                cross = jnp.dot(qx[:, qs], states[h].astype(BF16), preferred_element_type=F32)
                states[h] = chunk_decay[h] * states[h] + lax.dot_general(
                    kz[:, qs], vh, TN_DIMS, preferred_element_type=F32)
                ret = inner + cross
                mu = jnp.mean(ret, axis=-1, keepdims=True)
                cen = ret - mu
                var = jnp.mean(cen * cen, axis=-1, keepdims=True)
                outs.append(cen * lax.rsqrt(var + GN_EPS))
            normed = jnp.concatenate(outs, axis=-1) * gnw_ref[...]
            o_ref[pl.ds(r0, c), :] = (g * jax.nn.sigmoid(g) * normed).astype(BF16)
        for h in range(RET_HEADS):
            state_ref[h] = states[h]
        return carry

    lax.fori_loop(0, n_chunks // RET_UNROLL, chunk_group, 0)


def _retention_consts():
    h = np.arange(RET_HEADS, dtype=np.float32)
    log_g = np.log1p(-np.exp2(-5.0 - h)).astype(np.float32)
    idx = np.arange(RET_CHUNK, dtype=np.float32)
    rel = idx[:, None] - idx[None, :]
    scale = np.float32(RET_DK ** -0.5)
    decay = np.where(rel >= 0, np.exp(log_g[:, None, None] * np.maximum(rel, 0.0)), 0.0).astype(np.float32) * scale
    zeta = np.exp(log_g[:, None] * (RET_CHUNK - 1.0 - idx)).astype(np.float32) * scale
    xi = np.exp(log_g[:, None] * (idx + 1.0)).astype(np.float32)
    spread = lambda t: np.repeat(t.T, RET_DK, axis=1)
    chunk_decay = tuple(float(v) for v in np.exp(log_g * RET_CHUNK).astype(np.float32))
    return decay, spread(xi), spread(zeta), chunk_decay


def _retention(rq, rk, rv, rg, gn_w, batch, seq):
    n = rq.shape[0]
    decay, xi, zeta, chunk_decay = _retention_consts()
    gnw = gn_w.reshape(1, RET_V_W).astype(F32)
    seq_blk = lambda w: pl.BlockSpec((seq, w), lambda b: (b, 0))
    full = lambda a: pl.BlockSpec(a.shape, lambda b: (0,) * a.ndim)
    kern = functools.partial(_retention_kernel, n_chunks=seq // RET_CHUNK, chunk_decay=chunk_decay)
    return pl.pallas_call(
        kern,
        grid=(batch,),
        in_specs=[seq_blk(RET_QK_W), seq_blk(RET_QK_W), seq_blk(RET_V_W), seq_blk(RET_V_W),
                  full(decay), full(xi), full(zeta), full(gnw)],
        out_specs=seq_blk(RET_V_W),
        out_shape=jax.ShapeDtypeStruct((n, RET_V_W), BF16),
        scratch_shapes=[pltpu.VMEM((RET_HEADS, RET_DK, RET_DV), F32)],
        compiler_params=_params(("parallel",)),
        name="retention",
    )(rq, rk, rv, rg, jnp.asarray(decay), jnp.asarray(xi), jnp.asarray(zeta), gnw)


HEADS_PER_STEP = LANES // MOBA_DH
MOBA_Q_SCALE = MOBA_DH ** -0.5 * float(np.log2(np.e))


def _moba_kernel(q_ref, k_ref, vt_ref, o_ref, *, n_blocks):
    blk = MOBA_BLOCK
    dh = MOBA_DH
    blk_id = lax.broadcasted_iota(jnp.int32, (n_blocks, blk), 0)
    key_pos = lax.broadcasted_iota(jnp.int32, (blk, blk), 0)
    qry_pos = lax.broadcasted_iota(jnp.int32, (blk, blk), 1)
    causal = key_pos <= qry_pos
    rows = lambda b: slice(b * blk, (b + 1) * blk)

    class Unit:
        def __init__(self, hh, j, k_mean):
            self.hh, self.j = hh, j
            self.lanes = slice(hh * dh, (hh + 1) * dh)
            self.q = q_ref[rows(j), self.lanes]
            self.s_blocks, self.p_blocks = [], []
            self.m = self.l = None
            if j > 0:
                gate = lax.dot_general(k_mean, self.q.astype(F32), NT_DIMS, preferred_element_type=F32)
                past = blk_id < j
                gate = jnp.where(past, gate, NEG_INF)
                rank = jnp.zeros((n_blocks, blk), jnp.int32)
                for other in range(j):
                    g_o = gate[other:other + 1, :]
                    ahead = (g_o > gate) | ((g_o == gate) & (other < blk_id))
                    rank = rank + ahead.astype(jnp.int32)
                self.bias = jnp.where(past & (rank < MOBA_TOPK), 0.0, NEG_INF)

        def score_piece(self, i):
            s = lax.dot_general(k_ref[rows(i), self.lanes], self.q, NT_DIMS, preferred_element_type=F32)
            s = jnp.where(causal, s, NEG_INF) if i == self.j else s + self.bias[i:i + 1, :]
            col_max = jnp.max(s, axis=0, keepdims=True)
            self.m = col_max if self.m is None else jnp.maximum(self.m, col_max)
            self.s_blocks.append(s)

        def softmax_piece(self, i):
            p = jnp.exp2(self.s_blocks[i] - self.m)
            col_sum = jnp.sum(p, axis=0, keepdims=True)
            self.l = col_sum if self.l is None else self.l + col_sum
            self.p_blocks.append(p.astype(BF16))

        def output(self):
            p_all = jnp.concatenate(self.p_blocks, axis=0)
            acc = jnp.dot(vt_ref[0, self.lanes, 0:(self.j + 1) * blk], p_all, preferred_element_type=F32)
            return (acc / self.l).T

    k_means = []
    for hh in range(HEADS_PER_STEP):
        kf = k_ref[:, hh * dh:(hh + 1) * dh].astype(F32)
        k_means.append(jnp.mean(kf.reshape(n_blocks, blk, dh), axis=1))
    order = [(j, hh) for j in range(n_blocks) for hh in range(HEADS_PER_STEP)]
    units, outs = {}, {}
    for step in range(len(order) + 2):
        scoring = softmaxing = None
        if step < len(order):
            j, hh = order[step]
            scoring = units[step] = Unit(hh, j, k_means[hh])
        if 0 <= step - 1 < len(order):
            softmaxing = units[step - 1]
        for i in range(n_blocks):
            if scoring is not None and i <= scoring.j:
                scoring.score_piece(i)
            if softmaxing is not None and i <= softmaxing.j:
                softmaxing.softmax_piece(i)
        if 0 <= step - 2 < len(order):
            done = units.pop(step - 2)
            outs[(done.j, done.hh)] = done.output()
            if done.hh == HEADS_PER_STEP - 1:
                o_ref[rows(done.j), :] = jnp.concatenate(
                    [outs.pop((done.j, h)) for h in range(HEADS_PER_STEP)], axis=-1).astype(BF16)


def _moba(mq, mk, mvt, batch, seq):
    n = mq.shape[0]
    n_blocks = seq // MOBA_BLOCK
    pairs = MOBA_HEADS // HEADS_PER_STEP
    qk_spec = pl.BlockSpec((seq, LANES), lambda b, p: (b, p))
    kern = functools.partial(_moba_kernel, n_blocks=n_blocks)
    return pl.pallas_call(
        kern,
        grid=(batch, pairs),
        in_specs=[qk_spec, qk_spec, pl.BlockSpec((1, LANES, seq), lambda b, p: (b, p, 0))],
        out_specs=qk_spec,
        out_shape=jax.ShapeDtypeStruct((n, MOBA_W), BF16),
        compiler_params=_params(("parallel", "parallel")),
        name="moba",
    )(mq, mk, mvt)


def _store_row_tiled(ref, val):
    rows = val.shape[0]
    for s in range(ROW_TILES):
        ref[pl.ds(s, rows, stride=ROW_TILES), :] = val[:, s * LANES:(s + 1) * LANES]


def _load_row_tiled(ref, rows, lead=()):
    parts = []
    for s in range(ROW_TILES):
        parts.append(ref[tuple(lead) + (pl.ds(s, rows, stride=ROW_TILES), slice(None))])
    return jnp.concatenate(parts, axis=-1)


def _route_kernel(x_ref, ret_ref, moba_ref, wor_ref, wom_ref, nw_ref, wrt_ref, br_ref, tri_ref,
                  x1_ref, h2_ref, te_ref, gate_ref, rank_ref, cnt_ref, carry_ref):
    @pl.when(pl.program_id(0) == 0)
    def _():
        carry_ref[...] = jnp.zeros_like(carry_ref)

    x1 = (x_ref[...]
          + jnp.dot(ret_ref[...], wor_ref[...], preferred_element_type=F32)
          + jnp.dot(moba_ref[...], wom_ref[...], preferred_element_type=F32))
    x1_ref[...] = x1
    ms = jnp.mean(x1 * x1, axis=-1, keepdims=True)
    h2 = x1 * lax.rsqrt(ms + RMS_EPS) * nw_ref[...]
    _store_row_tiled(h2_ref, h2)
    logits = lax.dot_general(wrt_ref[...], h2, NT_DIMS, precision=lax.Precision.HIGHEST,
                             preferred_element_type=F32) + br_ref[...]
    tm = logits.shape[1]
    e_id = lax.broadcasted_iota(jnp.int32, (N_EXPERTS, tm), 0)
    vals, ids = [], []
    for _ in range(TOP_K):
        m = jnp.max(logits, axis=0, keepdims=True)
        idx = jnp.min(jnp.where(logits == m, e_id, N_EXPERTS), axis=0, keepdims=True)
        vals.append(m)
        ids.append(idx)
        logits = jnp.where(e_id == idx, -jnp.inf, logits)
    exps = [jnp.exp(v - vals[0]) for v in vals]
    denom = exps[0] + exps[1] + exps[2] + exps[3]
    gate_ref[...] = jnp.concatenate([e / denom for e in exps], axis=0)
    te_ref[...] = jnp.concatenate(ids, axis=0)
    hits = [e_id == idx for idx in ids]
    picked = (hits[0] | hits[1] | hits[2] | hits[3])
    picked_f = jnp.where(picked, 1.0, 0.0)
    before = jnp.dot(picked_f.astype(BF16), tri_ref[...], preferred_element_type=F32) + carry_ref[...]
    ranks = [jnp.sum(jnp.where(hit, before, 0.0), axis=0, keepdims=True) for hit in hits]
    rank_ref[...] = jnp.concatenate(ranks, axis=0).astype(jnp.int32)
    carry_ref[...] = carry_ref[...] + jnp.sum(picked_f, axis=1, keepdims=True)
    cnt_ref[...] = jnp.broadcast_to(carry_ref[...], cnt_ref.shape)


def _route(x2, ret, moba, w_out, norm_w, w_router, b_router, tm):
    n = x2.shape[0]
    wo_r = w_out[:RET_V_W].astype(BF16)
    wo_m = w_out[RET_V_W:].astype(BF16)
    nw = norm_w.reshape(1, D_MODEL)
    wrt = w_router.T.astype(F32)
    br = b_router.reshape(N_EXPERTS, 1).astype(F32)
    tri = (np.arange(tm)[:, None] < np.arange(tm)[None, :]).astype(np.float32)
    tri = jnp.asarray(tri, dtype=BF16)
    row = lambda w: pl.BlockSpec((tm, w), lambda i: (i, 0))
    col = pl.BlockSpec((TOP_K, tm), lambda i: (0, i))
    full = lambda a: pl.BlockSpec(a.shape, lambda i: (0,) * a.ndim)
    out_shape = (
        jax.ShapeDtypeStruct((n, D_MODEL), F32),
        jax.ShapeDtypeStruct((n * ROW_TILES, LANES), F32),
        jax.ShapeDtypeStruct((TOP_K, n), jnp.int32),
        jax.ShapeDtypeStruct((TOP_K, n), F32),
        jax.ShapeDtypeStruct((TOP_K, n), jnp.int32),
        jax.ShapeDtypeStruct((N_EXPERTS, LANES), F32),
    )
    return pl.pallas_call(
        _route_kernel,
        grid=(n // tm,),
        in_specs=[row(D_MODEL), row(RET_V_W), row(MOBA_W), full(wo_r), full(wo_m), full(nw), full(wrt),
                  full(br), full(tri)],
        out_specs=(row(D_MODEL), pl.BlockSpec((tm * ROW_TILES, LANES), lambda i: (i, 0)), col, col, col,
                   pl.BlockSpec((N_EXPERTS, LANES), lambda i: (0, 0))),
        out_shape=out_shape,
        scratch_shapes=[pltpu.VMEM((N_EXPERTS, 1), F32)],
        compiler_params=_params(("arbitrary",)),
        name="out_proj_route",
    )(x2, ret, moba, wo_r, wo_m, nw, wrt, br, tri)


def _row_slice(ref, row):
    return ref.at[pl.ds(pl.multiple_of(row * ROW_TILES, ROW_TILES), ROW_TILES)]


DISPATCH_UNROLL = 4


def _dispatch_kernel(fill_lo_ref, fill_n_ref, dest_ref, h2_ref, xs_ref, zero_ref, sem, fill_sem, *, tt):
    @pl.when(pl.program_id(0) == 0)
    def _():
        zero_ref[...] = jnp.zeros_like(zero_ref)

        def fill_copy(e, r):
            return pltpu.make_async_copy(zero_ref, _row_slice(xs_ref, fill_lo_ref[e] + r), fill_sem)

        def fill(e, carry):
            def one(r, c):
                fill_copy(e, r).start()
                return c
            return lax.fori_loop(0, fill_n_ref[e], one, carry)

        def fill_wait(e, carry):
            def one(r, c):
                fill_copy(e, r).wait()
                return c
            return lax.fori_loop(0, fill_n_ref[e], one, carry)

        lax.fori_loop(0, N_EXPERTS, fill, 0)
        lax.fori_loop(0, N_EXPERTS, fill_wait, 0)

    def row_copy(t, k):
        return pltpu.make_async_copy(_row_slice(h2_ref, t), _row_slice(xs_ref, dest_ref[0, 0, k * tt + t]), sem)

    def issue(g, carry):
        for u in range(DISPATCH_UNROLL):
            for k in range(TOP_K):
                row_copy(g * DISPATCH_UNROLL + u, k).start(priority=k % 2)
        return carry

    lax.fori_loop(0, tt // DISPATCH_UNROLL, issue, 0)

    def drain(g, carry):
        for u in range(DISPATCH_UNROLL):
            for k in range(TOP_K):
                row_copy(g * DISPATCH_UNROLL + u, k).wait()
        return carry

    lax.fori_loop(0, tt // DISPATCH_UNROLL, drain, 0)


def _dispatch(h2_tiled, dest, fill_lo, fill_n, n_slots, tt):
    n = dest.shape[1]
    tiles = n // tt
    dest_t = dest.reshape(TOP_K, tiles, tt).transpose(1, 0, 2).reshape(tiles, 1, TOP_K * tt)
    grid_spec = pltpu.PrefetchScalarGridSpec(
        num_scalar_prefetch=2,
        grid=(tiles,),
        in_specs=[pl.BlockSpec((1, 1, TOP_K * tt), lambda i, lo, cnt: (i, 0, 0), memory_space=pltpu.SMEM),
                  pl.BlockSpec((tt * ROW_TILES, LANES), lambda i, lo, cnt: (i, 0))],
        out_specs=pl.BlockSpec(memory_space=pl.ANY),
        scratch_shapes=[pltpu.VMEM((ROW_TILES, LANES), F32), pltpu.SemaphoreType.DMA(()),
                        pltpu.SemaphoreType.DMA(())],
    )
    return pl.pallas_call(
        functools.partial(_dispatch_kernel, tt=tt),
        grid_spec=grid_spec,
        out_shape=jax.ShapeDtypeStruct((n_slots * ROW_TILES, LANES), F32),
        compiler_params=_params(("arbitrary",)),
        name="dispatch",
    )(fill_lo, fill_n, dest_t, h2_tiled)


def _expert_kernel(be_ref, used_ref, xs_ref, w1_ref, b1_ref, w2_ref, b2_ref, y_ref, w1b_ref, w2b_ref, prev_ref,
                   *, mb):
    b = pl.program_id(0)

    @pl.when(b == 0)
    def _():
        prev_ref[...] = jnp.zeros_like(prev_ref)

    @pl.when((b == 0) | (be_ref[b] != be_ref[jnp.maximum(b - 1, 0)]))
    def _():
        w1b_ref[...] = w1_ref[0].astype(BF16)
        w2b_ref[...] = w2_ref[0].astype(BF16)

    @pl.when(b < used_ref[0])
    def _():
        x = _load_row_tiled(xs_ref, mb).astype(BF16)
        u = jnp.dot(x, w1b_ref[...], preferred_element_type=F32) + b1_ref[0]
        _store_row_tiled(y_ref, prev_ref[...])
        gate = jnp.minimum(u[:, :D_EXPERT], SWIGLU_LIMIT)
        lin = jnp.clip(u[:, D_EXPERT:], -SWIGLU_LIMIT, SWIGLU_LIMIT)
        act = gate * jax.nn.sigmoid(SWIGLU_ALPHA * gate) * (lin + 1.0)
        prev_ref[...] = jnp.dot(act.astype(BF16), w2b_ref[...], preferred_element_type=F32) + b2_ref[0]

    @pl.when(b >= used_ref[0])
    def _():
        _store_row_tiled(y_ref, prev_ref[...])
        prev_ref[...] = jnp.zeros_like(prev_ref)


def _experts(xs, block_expert, n_used, w1, b1, w2, b2, mb):
    n_blocks = block_expert.shape[0]
    block_expert = jnp.concatenate([block_expert, block_expert[-1:]])
    b1r = b1.reshape(N_EXPERTS, 1, 2 * D_EXPERT).astype(F32)
    b2r = b2.reshape(N_EXPERTS, 1, D_MODEL).astype(F32)
    grid_spec = pltpu.PrefetchScalarGridSpec(
        num_scalar_prefetch=2,
        grid=(n_blocks + 1,),
        in_specs=[
            pl.BlockSpec((mb * ROW_TILES, LANES), lambda i, be, used: (jnp.minimum(i, used[0] - 1), 0)),
            pl.BlockSpec((1, D_MODEL, 2 * D_EXPERT), lambda i, be, used: (be[i], 0, 0)),
            pl.BlockSpec((1, 1, 2 * D_EXPERT), lambda i, be, used: (be[i], 0, 0)),
            pl.BlockSpec((1, D_EXPERT, D_MODEL), lambda i, be, used: (be[i], 0, 0)),
            pl.BlockSpec((1, 1, D_MODEL), lambda i, be, used: (be[i], 0, 0)),
        ],
        out_specs=pl.BlockSpec((mb * ROW_TILES, LANES), lambda i, be, used: (jnp.maximum(i - 1, 0), 0)),
        scratch_shapes=[pltpu.VMEM((D_MODEL, 2 * D_EXPERT), BF16), pltpu.VMEM((D_EXPERT, D_MODEL), BF16),
                        pltpu.VMEM((mb, D_MODEL), F32)],
    )
    return pl.pallas_call(
        functools.partial(_expert_kernel, mb=mb),
        grid_spec=grid_spec,
        out_shape=jax.ShapeDtypeStruct(xs.shape, F32),
        compiler_params=_params(("arbitrary",), vmem=EXPERT_VMEM_LIMIT),
        name="experts",
    )(block_expert, n_used, xs, w1, b1r, w2, b2r)


COMBINE_UNROLL = 4
COMBINE_CHUNK = 64


def _combine_kernel(dest_ref, dest_next_ref, x1_ref, gate_ref, fw_ref, y_ref, o_ref, buf_ref, sems, *, tc, tiles):
    i = pl.program_id(0)
    slot = i % 2

    def row_copy(d_ref, dst_slot, t, k):
        return pltpu.make_async_copy(_row_slice(y_ref, d_ref[0, 0, k * tc + t]),
                                     _row_slice(buf_ref.at[dst_slot, k], t), sems.at[dst_slot])

    def gather(d_ref, dst_slot):
        def body(g, carry):
            for u in range(COMBINE_UNROLL):
                for k in range(TOP_K):
                    row_copy(d_ref, dst_slot, g * COMBINE_UNROLL + u, k).start(priority=k % 2)
            return carry
        lax.fori_loop(0, tc // COMBINE_UNROLL, body, 0)

    @pl.when(i == 0)
    def _():
        gather(dest_ref, 0)

    def drain(dst_slot):
        def body(g, carry):
            for u in range(COMBINE_UNROLL):
                for k in range(TOP_K):
                    row_copy(dest_ref, dst_slot, g * COMBINE_UNROLL + u, k).wait()
            return carry
        lax.fori_loop(0, tc // COMBINE_UNROLL, body, 0)

    drain(slot)

    gates = gate_ref[...]
    gates_t = jnp.concatenate([gates, jnp.zeros((SUBLANES - TOP_K, tc), F32)], axis=0).T
    pieces = (tc // COMBINE_CHUNK) * ROW_TILES
    per_piece = tc // pieces
    piece = 0
    for c in range(tc // COMBINE_CHUNK):
        tok = slice(c * COMBINE_CHUNK, (c + 1) * COMBINE_CHUNK)
        gate_cols = [jnp.broadcast_to(gates_t[tok, k:k + 1], (COMBINE_CHUNK, LANES)) for k in range(TOP_K)]
        sum_sq = jnp.zeros((COMBINE_CHUNK, 1), F32)
        for s in range(ROW_TILES):
            slab = slice(s * LANES, (s + 1) * LANES)
            xo = x1_ref[tok, slab]
            for k in range(TOP_K):
                rows = pl.ds(c * COMBINE_CHUNK * ROW_TILES + s, COMBINE_CHUNK, stride=ROW_TILES)
                xo = xo + gate_cols[k] * buf_ref[slot, k, rows, :]
            o_ref[tok, slab] = xo
            sum_sq = sum_sq + jnp.sum(xo * xo, axis=-1, keepdims=True)
            for t in range(piece * per_piece, (piece + 1) * per_piece):
                for k in range(TOP_K):
                    row_copy(dest_next_ref, 1 - slot, t, k).start(priority=k % 2)
            piece += 1
        inv = lax.rsqrt(sum_sq * (1.0 / D_MODEL) + RMS_EPS)
        o_ref[tok, :] = o_ref[tok, :] * inv * fw_ref[...]

    @pl.when(i == tiles - 1)
    def _():
        drain(1 - slot)


def _combine(x1, gates, dest, y, final_w, tc):
    n = x1.shape[0]
    tiles = n // tc
    dest_t = dest.reshape(TOP_K, tiles, tc).transpose(1, 0, 2).reshape(tiles, 1, TOP_K * tc)
    fw = final_w.reshape(1, D_MODEL)
    dest_spec = lambda step: pl.BlockSpec((1, 1, TOP_K * tc), lambda i: (jnp.minimum(i + step, tiles - 1), 0, 0),
                                          memory_space=pltpu.SMEM)
    return pl.pallas_call(
        functools.partial(_combine_kernel, tc=tc, tiles=tiles),
        grid=(tiles,),
        in_specs=[dest_spec(0), dest_spec(1),
                  pl.BlockSpec((tc, D_MODEL), lambda i: (i, 0)),
                  pl.BlockSpec((TOP_K, tc), lambda i: (0, i)),
                  pl.BlockSpec((1, D_MODEL), lambda i: (0, 0)),
                  pl.BlockSpec(memory_space=pl.ANY)],
        out_specs=pl.BlockSpec((tc, D_MODEL), lambda i: (i, 0)),
        out_shape=jax.ShapeDtypeStruct((n, D_MODEL), F32),
        scratch_shapes=[pltpu.VMEM((2, TOP_K, tc * ROW_TILES, LANES), F32), pltpu.SemaphoreType.DMA((2,))],
        compiler_params=_params(("arbitrary",)),
        name="combine",
    )(dest_t, dest_t, x1, gates, fw, y)


PROJ_ROWS = 512
EXPERT_ROWS = 512
DISPATCH_ROWS = 2048
COMBINE_ROWS = 256


def _slot_layout(counts, n_pairs, mb):
    padded = (counts + mb - 1) // mb * mb
    pad_end = jnp.cumsum(padded)
    pad_start = pad_end - padded
    n_blocks = n_pairs // mb + N_EXPERTS
    block_row0 = jnp.arange(n_blocks, dtype=jnp.int32) * mb
    block_expert = jnp.minimum(
        jnp.sum((pad_end[None, :] <= block_row0[:, None]).astype(jnp.int32), axis=1), N_EXPERTS - 1)
    n_used = (pad_end[-1:] // mb).astype(jnp.int32)
    return pad_start.astype(jnp.int32), (pad_start + counts).astype(jnp.int32), (padded - counts).astype(jnp.int32), \
        block_expert, n_used, n_blocks


def _layer(x2, batch, seq, norm1_w, w_in, ret_norm_w, w_out, norm2_w, w_router, b_router, w1, b1, w2, b2,
           final_w):
    n = x2.shape[0]
    tm = min(PROJ_ROWS, seq)
    rq, rk, rv, rg, mq, mk, mvt = _in_projection(x2, norm1_w, w_in, batch, seq, tm)
    ret = _retention(rq, rk, rv, rg, ret_norm_w, batch, seq)
    moba = _moba(mq, mk, mvt, batch, seq)
    x1, h2_tiled, top_e, gates, rank, cnt = _route(x2, ret, moba, w_out, norm2_w, w_router, b_router, tm)
    counts = cnt[:, 0].astype(jnp.int32)
    mb = EXPERT_ROWS
    pad_start, fill_lo, fill_n, block_expert, n_used, n_blocks = _slot_layout(counts, n * TOP_K, mb)
    expert_ids = jnp.arange(N_EXPERTS, dtype=jnp.int32)[:, None, None]
    dest = rank + jnp.sum(jnp.where(top_e[None] == expert_ids, pad_start[:, None, None], 0), axis=0)
    xs = _dispatch(h2_tiled, dest, fill_lo, fill_n, n_blocks * mb, min(DISPATCH_ROWS, n))
    y = _experts(xs, block_expert, n_used, w1, b1, w2, b2, mb)
    return _combine(x1, gates, dest, y, final_w, min(COMBINE_ROWS, n))


def kernel(x, norm1_w, w_in, ret_norm_w, w_out, norm2_w, w_router, b_router, w1, b1, w2, b2, final_norm_w):
    batch, seq, d = x.shape
    depth = w_in.shape[0]
    assert depth == 1 and d == D_MODEL and seq % MOBA_BLOCK == 0
    x2 = x.reshape(batch * seq, d)
    out = _layer(x2, batch, seq, norm1_w[0], w_in[0], ret_norm_w[0], w_out[0], norm2_w[0], w_router[0],
                 b_router[0], w1[0], b1[0], w2[0], b2[0], final_norm_w)
    return out.reshape(batch, seq, d)
```

```python
import functools

import numpy as np
import jax
import jax.numpy as jnp
from jax import lax
from jax.experimental import pallas as pl
from jax.experimental.pallas import tpu as pltpu

D_MODEL = 1024
RET_HEADS = 4
RET_DK = 64
RET_DV = 128
RET_CHUNK = 128
RET_THETA = 10000.0
MOBA_HEADS = 8
MOBA_DH = 64
MOBA_BLOCK = 256
MOBA_TOPK = 3
ROPE_THETA = 500000.0
ROPE_DIM = MOBA_DH // 4
RET_QK_W = RET_HEADS * RET_DK
RET_V_W = RET_HEADS * RET_DV
MOBA_W = MOBA_HEADS * MOBA_DH
N_EXPERTS = 32
TOP_K = 4
D_EXPERT = D_MODEL
SWIGLU_LIMIT = 7.0
SWIGLU_ALPHA = 1.702
RMS_EPS = 1e-5
GN_EPS = 1e-6
NEG_INF = -1e30

SUBLANES = 8
LANES = 128
ROW_TILES = D_MODEL // LANES
VMEM_LIMIT = 48 * 1024 * 1024
EXPERT_VMEM_LIMIT = 56 * 1024 * 1024

F32 = jnp.float32
BF16 = jnp.bfloat16
NT_DIMS = (((1,), (1,)), ((), ()))
TN_DIMS = (((0,), (0,)), ((), ()))


def _params(sem, vmem=VMEM_LIMIT):
    return pltpu.CompilerParams(dimension_semantics=sem, vmem_limit_bytes=vmem)


def _rotate(x, cos, sin_lo, sin_hi, half):
    width = x.shape[-1]
    nxt = pltpu.roll(x, width - half, axis=1)
    prv = pltpu.roll(x, half, axis=1)
    return x * cos + nxt * sin_lo + prv * sin_hi


def _inproj_kernel(x_ref, nw_ref, w_ref, wvt_ref, rc_ref, rlo_ref, rhi_ref, mc_ref, mlo_ref, mhi_ref,
                   rq_ref, rk_ref, rv_ref, rg_ref, mq_ref, mk_ref, mvt_ref):
    x = x_ref[...]
    ms = jnp.mean(x * x, axis=-1, keepdims=True)
    h = (x * lax.rsqrt(ms + RMS_EPS) * nw_ref[...]).astype(BF16)
    proj = jnp.dot(h, w_ref[...], preferred_element_type=F32)
    o = 0
    rq = proj[:, o:o + RET_QK_W]; o += RET_QK_W
    rk = proj[:, o:o + RET_QK_W]; o += RET_QK_W
    rv = proj[:, o:o + RET_V_W]; o += RET_V_W
    rg = proj[:, o:o + RET_V_W]; o += RET_V_W
    mq = proj[:, o:o + MOBA_W]; o += MOBA_W
    mk = proj[:, o:o + MOBA_W]
    rc, rlo, rhi = rc_ref[...], rlo_ref[...], rhi_ref[...]
    mc, mlo, mhi = mc_ref[...], mlo_ref[...], mhi_ref[...]
    rq_ref[...] = _rotate(rq, rc, rlo, rhi, RET_DK // 2).astype(BF16)
    rk_ref[...] = _rotate(rk, rc, rlo, rhi, RET_DK // 2).astype(BF16)
    rv_ref[...] = rv.astype(BF16)
    rg_ref[...] = rg.astype(BF16)
    mq_ref[...] = (_rotate(mq, mc, mlo, mhi, ROPE_DIM // 2) * MOBA_Q_SCALE).astype(BF16)
    mk_ref[...] = _rotate(mk, mc, mlo, mhi, ROPE_DIM // 2).astype(BF16)
    mvt_ref[0] = lax.dot_general(wvt_ref[...], h, NT_DIMS, preferred_element_type=F32).astype(BF16)


def _rotary_tables(seq, rot_dim, theta, head_dim, heads):
    half = rot_dim // 2
    inv_freq = (np.float32(theta) ** (np.float32(-2.0) * np.arange(half, dtype=np.float32) / np.float32(rot_dim)))
    ang = np.arange(seq, dtype=np.float32)[:, None] * inv_freq.astype(np.float32)[None, :]
    cos, sin = np.cos(ang).astype(np.float32), np.sin(ang).astype(np.float32)
    rest = head_dim - rot_dim
    ones = np.ones((seq, rest), np.float32)
    zeros = np.zeros((seq, rest), np.float32)
    zh = np.zeros((seq, half), np.float32)
    cos_t = np.concatenate([cos, cos, ones], axis=-1)
    lo_t = np.concatenate([-sin, zh, zeros], axis=-1)
    hi_t = np.concatenate([zh, sin, zeros], axis=-1)
    tile = lambda t: jnp.asarray(np.tile(t, (1, heads)))
    return tile(cos_t), tile(lo_t), tile(hi_t)


def _in_projection(x2, norm_w, w_in, batch, seq, tm):
    n = x2.shape[0]
    st = seq // tm
    main_w = 2 * RET_QK_W + 2 * RET_V_W + 2 * MOBA_W
    w_main = w_in[:, :main_w].astype(BF16)
    w_vt = w_in[:, main_w:].T.astype(BF16)
    r_tabs = _rotary_tables(seq, RET_DK, RET_THETA, RET_DK, RET_HEADS)
    m_tabs = _rotary_tables(seq, ROPE_DIM, ROPE_THETA, MOBA_DH, MOBA_HEADS)
    row = lambda w: pl.BlockSpec((tm, w), lambda b, s: (b * st + s, 0))
    pos = lambda w: pl.BlockSpec((tm, w), lambda b, s: (s, 0))
    full = lambda a: pl.BlockSpec(a.shape, lambda b, s: (0,) * a.ndim)
    nw = norm_w.reshape(1, D_MODEL)
    out_shape = (
        jax.ShapeDtypeStruct((n, RET_QK_W), BF16), jax.ShapeDtypeStruct((n, RET_QK_W), BF16),
        jax.ShapeDtypeStruct((n, RET_V_W), BF16), jax.ShapeDtypeStruct((n, RET_V_W), BF16),
        jax.ShapeDtypeStruct((n, MOBA_W), BF16), jax.ShapeDtypeStruct((n, MOBA_W), BF16),
        jax.ShapeDtypeStruct((batch, MOBA_W, seq), BF16),
    )
    return pl.pallas_call(
        _inproj_kernel,
        grid=(batch, st),
        in_specs=[row(D_MODEL), full(nw), full(w_main), full(w_vt),
                  pos(RET_QK_W), pos(RET_QK_W), pos(RET_QK_W), pos(MOBA_W), pos(MOBA_W), pos(MOBA_W)],
        out_specs=(row(RET_QK_W), row(RET_QK_W), row(RET_V_W), row(RET_V_W), row(MOBA_W), row(MOBA_W),
                   pl.BlockSpec((1, MOBA_W, tm), lambda b, s: (b, 0, s))),
        out_shape=out_shape,
        compiler_params=_params(("parallel", "parallel")),
        name="in_projection",
    )(x2, nw, w_main, w_vt, *r_tabs, *m_tabs)


RET_UNROLL = 4


def _retention_kernel(q_ref, k_ref, v_ref, g_ref, decay_ref, xi_ref, zeta_ref, gnw_ref, o_ref, state_ref,
                      *, n_chunks, chunk_decay):
    c = RET_CHUNK
    state_ref[...] = jnp.zeros_like(state_ref)

    def chunk_group(gi, carry):
        states = [state_ref[h] for h in range(RET_HEADS)]
        for u in range(RET_UNROLL):
            r0 = pl.multiple_of((gi * RET_UNROLL + u) * c, c)
            q = q_ref[pl.ds(r0, c), :]
            k = k_ref[pl.ds(r0, c), :]
            v = v_ref[pl.ds(r0, c), :]
            g = g_ref[pl.ds(r0, c), :].astype(F32)
            qx = (q.astype(F32) * xi_ref[...]).astype(BF16)
            kz = (k.astype(F32) * zeta_ref[...]).astype(BF16)
            outs = []
            for h in range(RET_HEADS):
                qs = slice(h * RET_DK, (h + 1) * RET_DK)
                vs = slice(h * RET_DV, (h + 1) * RET_DV)
                vh = v[:, vs]
                scores = lax.dot_general(q[:, qs], k[:, qs], NT_DIMS, preferred_element_type=F32) * decay_ref[h]
                inner = jnp.dot(scores.astype(BF16), vh, preferred_element_type=F32)
                cross = jnp.dot(qx[:, qs], states[h].astype(BF16), preferred_element_type=F32)
                states[h] = chunk_decay[h] * states[h] + lax.dot_general(
                    kz[:, qs], vh, TN_DIMS, preferred_element_type=F32)
                ret = inner + cross
                mu = jnp.mean(ret, axis=-1, keepdims=True)
                cen = ret - mu
                var = jnp.mean(cen * cen, axis=-1, keepdims=True)
                outs.append(cen * lax.rsqrt(var + GN_EPS))
            normed = jnp.concatenate(outs, axis=-1) * gnw_ref[...]
            o_ref[pl.ds(r0, c), :] = (g * jax.nn.sigmoid(g) * normed).astype(BF16)
        for h in range(RET_HEADS):
            state_ref[h] = states[h]
        return carry

    lax.fori_loop(0, n_chunks // RET_UNROLL, chunk_group, 0)


def _retention_consts():
    h = np.arange(RET_HEADS, dtype=np.float32)
    log_g = np.log1p(-np.exp2(-5.0 - h)).astype(np.float32)
    idx = np.arange(RET_CHUNK, dtype=np.float32)
    rel = idx[:, None] - idx[None, :]
    scale = np.float32(RET_DK ** -0.5)
    decay = np.where(rel >= 0, np.exp(log_g[:, None, None] * np.maximum(rel, 0.0)), 0.0).astype(np.float32) * scale
    zeta = np.exp(log_g[:, None] * (RET_CHUNK - 1.0 - idx)).astype(np.float32) * scale
    xi = np.exp(log_g[:, None] * (idx + 1.0)).astype(np.float32)
    spread = lambda t: np.repeat(t.T, RET_DK, axis=1)
    chunk_decay = tuple(float(v) for v in np.exp(log_g * RET_CHUNK).astype(np.float32))
    return decay, spread(xi), spread(zeta), chunk_decay


def _retention(rq, rk, rv, rg, gn_w, batch, seq):
    n = rq.shape[0]
    decay, xi, zeta, chunk_decay = _retention_consts()
    gnw = gn_w.reshape(1, RET_V_W).astype(F32)
    seq_blk = lambda w: pl.BlockSpec((seq, w), lambda b: (b, 0))
    full = lambda a: pl.BlockSpec(a.shape, lambda b: (0,) * a.ndim)
    kern = functools.partial(_retention_kernel, n_chunks=seq // RET_CHUNK, chunk_decay=chunk_decay)
    return pl.pallas_call(
        kern,
        grid=(batch,),
        in_specs=[seq_blk(RET_QK_W), seq_blk(RET_QK_W), seq_blk(RET_V_W), seq_blk(RET_V_W),
                  full(decay), full(xi), full(zeta), full(gnw)],
        out_specs=seq_blk(RET_V_W),
        out_shape=jax.ShapeDtypeStruct((n, RET_V_W), BF16),
        scratch_shapes=[pltpu.VMEM((RET_HEADS, RET_DK, RET_DV), F32)],
        compiler_params=_params(("parallel",)),
        name="retention",
    )(rq, rk, rv, rg, jnp.asarray(decay), jnp.asarray(xi), jnp.asarray(zeta), gnw)


HEADS_PER_STEP = LANES // MOBA_DH
MOBA_Q_SCALE = MOBA_DH ** -0.5 * float(np.log2(np.e))
ONES_ROWS = 16


def _moba_kernel(q_ref, k_ref, vt_ref, o_ref, *, n_blocks):
    blk = MOBA_BLOCK
    dh = MOBA_DH
    blk_id = lax.broadcasted_iota(jnp.int32, (n_blocks, blk), 0)
    key_pos = lax.broadcasted_iota(jnp.int32, (blk, blk), 0)
    qry_pos = lax.broadcasted_iota(jnp.int32, (blk, blk), 1)
    causal = key_pos <= qry_pos
    rows = lambda b: slice(b * blk, (b + 1) * blk)

    def masked_scores(hh, j, k_mean):
        lanes = slice(hh * dh, (hh + 1) * dh)
        q = q_ref[rows(j), lanes]
        n_keys = (j + 1) * blk
        s_all = lax.dot_general(k_ref[0:n_keys, lanes], q, NT_DIMS, preferred_element_type=F32)
        if j > 0:
            gate = lax.dot_general(k_mean, q.astype(F32), NT_DIMS, preferred_element_type=F32)
            past = blk_id < j
            gate = jnp.where(past, gate, NEG_INF)
            rank = jnp.zeros((n_blocks, blk), jnp.int32)
            for other in range(j):
                g_o = gate[other:other + 1, :]
                ahead = (g_o > gate) | ((g_o == gate) & (other < blk_id))
                rank = rank + ahead.astype(jnp.int32)
            bias = jnp.where(past & (rank < MOBA_TOPK), 0.0, NEG_INF)
        s_blocks = [s_all[rows(i), :] + bias[i:i + 1, :] for i in range(j)]
        s_blocks.append(jnp.where(causal, s_all[rows(j), :], NEG_INF))
        m = jnp.max(s_blocks[0], axis=0, keepdims=True)
        for s in s_blocks[1:]:
            m = jnp.maximum(m, jnp.max(s, axis=0, keepdims=True))
        return s_blocks, m

    def softmax_weights(s_blocks, m):
        return jnp.concatenate([jnp.exp2(s - m).astype(BF16) for s in s_blocks], axis=0)

    def weighted_values(hh, j, p_all):
        lanes = slice(hh * dh, (hh + 1) * dh)
        n_keys = (j + 1) * blk
        values = jnp.concatenate([vt_ref[0, lanes, 0:n_keys], jnp.ones((ONES_ROWS, n_keys), BF16)], axis=0)
        acc = jnp.dot(values, p_all, preferred_element_type=F32)
        return (acc[:dh] / acc[dh:dh + 1]).T

    k_means = []
    for hh in range(HEADS_PER_STEP):
        kf = k_ref[:, hh * dh:(hh + 1) * dh].astype(F32)
        k_means.append(jnp.mean(kf.reshape(n_blocks, blk, dh), axis=1))
    units = [(j, hh) for j in range(n_blocks) for hh in range(HEADS_PER_STEP)]
    scored, weighted, outs = {}, {}, {}
    for step in range(len(units) + 2):
        if step < len(units):
            j, hh = units[step]
            scored[step] = masked_scores(hh, j, k_means[hh])
        if 0 <= step - 1 < len(units):
            weighted[step - 1] = softmax_weights(*scored.pop(step - 1))
        if 0 <= step - 2 < len(units):
            j, hh = units[step - 2]
            outs[(j, hh)] = weighted_values(hh, j, weighted.pop(step - 2))
            if hh == HEADS_PER_STEP - 1:
                o_ref[rows(j), :] = jnp.concatenate(
                    [outs.pop((j, h)) for h in range(HEADS_PER_STEP)], axis=-1).astype(BF16)


def _moba(mq, mk, mvt, batch, seq):
    n = mq.shape[0]
    n_blocks = seq // MOBA_BLOCK
    pairs = MOBA_HEADS // HEADS_PER_STEP
    qk_spec = pl.BlockSpec((seq, LANES), lambda b, p: (b, p))
    kern = functools.partial(_moba_kernel, n_blocks=n_blocks)
    return pl.pallas_call(
        kern,
        grid=(batch, pairs),
        in_specs=[qk_spec, qk_spec, pl.BlockSpec((1, LANES, seq), lambda b, p: (b, p, 0))],
        out_specs=qk_spec,
        out_shape=jax.ShapeDtypeStruct((n, MOBA_W), BF16),
        compiler_params=_params(("parallel", "parallel")),
        name="moba",
    )(mq, mk, mvt)


def _store_row_tiled(ref, val):
    rows = val.shape[0]
    for s in range(ROW_TILES):
        ref[pl.ds(s, rows, stride=ROW_TILES), :] = val[:, s * LANES:(s + 1) * LANES]


def _load_row_tiled(ref, rows, lead=()):
    parts = []
    for s in range(ROW_TILES):
        parts.append(ref[tuple(lead) + (pl.ds(s, rows, stride=ROW_TILES), slice(None))])
    return jnp.concatenate(parts, axis=-1)


def _route_kernel(x_ref, ret_ref, moba_ref, wor_ref, wom_ref, nw_ref, wrt_ref, br_ref, tri_ref,
                  x1_ref, h2_ref, te_ref, gate_ref, rank_ref, cnt_ref, carry_ref):
    @pl.when(pl.program_id(0) == 0)
    def _():
        carry_ref[...] = jnp.zeros_like(carry_ref)

    x1 = (x_ref[...]
          + jnp.dot(ret_ref[...], wor_ref[...], preferred_element_type=F32)
          + jnp.dot(moba_ref[...], wom_ref[...], preferred_element_type=F32))
    x1_ref[...] = x1
    ms = jnp.mean(x1 * x1, axis=-1, keepdims=True)
    h2 = x1 * lax.rsqrt(ms + RMS_EPS) * nw_ref[...]
    _store_row_tiled(h2_ref, h2)
    logits = lax.dot_general(wrt_ref[...], h2, NT_DIMS, precision=lax.Precision.HIGHEST,
                             preferred_element_type=F32) + br_ref[...]
    tm = logits.shape[1]
    e_id = lax.broadcasted_iota(jnp.int32, (N_EXPERTS, tm), 0)
    vals, ids = [], []
    for _ in range(TOP_K):
        m = jnp.max(logits, axis=0, keepdims=True)
        idx = jnp.min(jnp.where(logits == m, e_id, N_EXPERTS), axis=0, keepdims=True)
        vals.append(m)
        ids.append(idx)
        logits = jnp.where(e_id == idx, -jnp.inf, logits)
    exps = [jnp.exp(v - vals[0]) for v in vals]
    denom = exps[0] + exps[1] + exps[2] + exps[3]
    gate_ref[...] = jnp.concatenate([e / denom for e in exps], axis=0)
    te_ref[...] = jnp.concatenate(ids, axis=0)
    hits = [e_id == idx for idx in ids]
    picked = (hits[0] | hits[1] | hits[2] | hits[3])
    picked_f = jnp.where(picked, 1.0, 0.0)
    before = jnp.dot(picked_f.astype(BF16), tri_ref[...], preferred_element_type=F32) + carry_ref[...]
    ranks = [jnp.sum(jnp.where(hit, before, 0.0), axis=0, keepdims=True) for hit in hits]
    rank_ref[...] = jnp.concatenate(ranks, axis=0).astype(jnp.int32)
    carry_ref[...] = carry_ref[...] + jnp.sum(picked_f, axis=1, keepdims=True)
    cnt_ref[...] = jnp.broadcast_to(carry_ref[...], cnt_ref.shape)


def _route(x2, ret, moba, w_out, norm_w, w_router, b_router, tm):
    n = x2.shape[0]
    wo_r = w_out[:RET_V_W].astype(BF16)
    wo_m = w_out[RET_V_W:].astype(BF16)
    nw = norm_w.reshape(1, D_MODEL)
    wrt = w_router.T.astype(F32)
    br = b_router.reshape(N_EXPERTS, 1).astype(F32)
    tri = (np.arange(tm)[:, None] < np.arange(tm)[None, :]).astype(np.float32)
    tri = jnp.asarray(tri, dtype=BF16)
    row = lambda w: pl.BlockSpec((tm, w), lambda i: (i, 0))
    col = pl.BlockSpec((TOP_K, tm), lambda i: (0, i))
    full = lambda a: pl.BlockSpec(a.shape, lambda i: (0,) * a.ndim)
    out_shape = (
        jax.ShapeDtypeStruct((n, D_MODEL), F32),
        jax.ShapeDtypeStruct((n * ROW_TILES, LANES), F32),
        jax.ShapeDtypeStruct((TOP_K, n), jnp.int32),
        jax.ShapeDtypeStruct((TOP_K, n), F32),
        jax.ShapeDtypeStruct((TOP_K, n), jnp.int32),
        jax.ShapeDtypeStruct((N_EXPERTS, LANES), F32),
    )
    return pl.pallas_call(
        _route_kernel,
        grid=(n // tm,),
        in_specs=[row(D_MODEL), row(RET_V_W), row(MOBA_W), full(wo_r), full(wo_m), full(nw), full(wrt),
                  full(br), full(tri)],
        out_specs=(row(D_MODEL), pl.BlockSpec((tm * ROW_TILES, LANES), lambda i: (i, 0)), col, col, col,
                   pl.BlockSpec((N_EXPERTS, LANES), lambda i: (0, 0))),
        out_shape=out_shape,
        scratch_shapes=[pltpu.VMEM((N_EXPERTS, 1), F32)],
        compiler_params=_params(("arbitrary",)),
        name="out_proj_route",
    )(x2, ret, moba, wo_r, wo_m, nw, wrt, br, tri)


def _row_slice(ref, row):
    return ref.at[pl.ds(pl.multiple_of(row * ROW_TILES, ROW_TILES), ROW_TILES)]


DISPATCH_UNROLL = 4


def _dispatch_kernel(fill_lo_ref, fill_n_ref, dest_ref, h2_ref, xs_ref, zero_ref, sem, fill_sem, *, tt):
    @pl.when(pl.program_id(0) == 0)
    def _():
        zero_ref[...] = jnp.zeros_like(zero_ref)

        def fill_copy(e, r):
            return pltpu.make_async_copy(zero_ref, _row_slice(xs_ref, fill_lo_ref[e] + r), fill_sem)

        def fill(e, carry):
            def one(r, c):
                fill_copy(e, r).start()
                return c
            return lax.fori_loop(0, fill_n_ref[e], one, carry)

        def fill_wait(e, carry):
            def one(r, c):
                fill_copy(e, r).wait()
                return c
            return lax.fori_loop(0, fill_n_ref[e], one, carry)

        lax.fori_loop(0, N_EXPERTS, fill, 0)
        lax.fori_loop(0, N_EXPERTS, fill_wait, 0)

    def row_copy(t, k):
        return pltpu.make_async_copy(_row_slice(h2_ref, t), _row_slice(xs_ref, dest_ref[0, 0, k * tt + t]), sem)

    def issue(g, carry):
        for u in range(DISPATCH_UNROLL):
            for k in range(TOP_K):
                row_copy(g * DISPATCH_UNROLL + u, k).start(priority=k % 2)
        return carry

    lax.fori_loop(0, tt // DISPATCH_UNROLL, issue, 0)

    def drain(g, carry):
        for u in range(DISPATCH_UNROLL):
            for k in range(TOP_K):
                row_copy(g * DISPATCH_UNROLL + u, k).wait()
        return carry

    lax.fori_loop(0, tt // DISPATCH_UNROLL, drain, 0)


def _dispatch(h2_tiled, dest, fill_lo, fill_n, n_slots, tt):
    n = dest.shape[1]
    tiles = n // tt
    dest_t = dest.reshape(TOP_K, tiles, tt).transpose(1, 0, 2).reshape(tiles, 1, TOP_K * tt)
    grid_spec = pltpu.PrefetchScalarGridSpec(
        num_scalar_prefetch=2,
        grid=(tiles,),
        in_specs=[pl.BlockSpec((1, 1, TOP_K * tt), lambda i, lo, cnt: (i, 0, 0), memory_space=pltpu.SMEM),
                  pl.BlockSpec((tt * ROW_TILES, LANES), lambda i, lo, cnt: (i, 0))],
        out_specs=pl.BlockSpec(memory_space=pl.ANY),
        scratch_shapes=[pltpu.VMEM((ROW_TILES, LANES), F32), pltpu.SemaphoreType.DMA(()),
                        pltpu.SemaphoreType.DMA(())],
    )
    return pl.pallas_call(
        functools.partial(_dispatch_kernel, tt=tt),
        grid_spec=grid_spec,
        out_shape=jax.ShapeDtypeStruct((n_slots * ROW_TILES, LANES), F32),
        compiler_params=_params(("arbitrary",)),
        name="dispatch",
    )(fill_lo, fill_n, dest_t, h2_tiled)


def _expert_kernel(be_ref, used_ref, xs_ref, w1_ref, b1_ref, w2_ref, b2_ref, y_ref, w1b_ref, w2b_ref, *, mb):
    b = pl.program_id(0)

    @pl.when((b == 0) | (be_ref[b] != be_ref[jnp.maximum(b - 1, 0)]))
    def _():
        w1b_ref[...] = w1_ref[0].astype(BF16)
        w2b_ref[...] = w2_ref[0].astype(BF16)

    @pl.when(b < used_ref[0])
    def _():
        x = _load_row_tiled(xs_ref, mb).astype(BF16)
        u = jnp.dot(x, w1b_ref[...], preferred_element_type=F32) + b1_ref[0]
        gate = jnp.minimum(u[:, :D_EXPERT], SWIGLU_LIMIT)
        lin = jnp.clip(u[:, D_EXPERT:], -SWIGLU_LIMIT, SWIGLU_LIMIT)
        act = gate * jax.nn.sigmoid(SWIGLU_ALPHA * gate) * (lin + 1.0)
        y = jnp.dot(act.astype(BF16), w2b_ref[...], preferred_element_type=F32) + b2_ref[0]
        _store_row_tiled(y_ref, y)

    @pl.when(b >= used_ref[0])
    def _():
        y_ref[...] = jnp.zeros_like(y_ref)


def _experts(xs, block_expert, n_used, w1, b1, w2, b2, mb):
    n_blocks = block_expert.shape[0]
    b1r = b1.reshape(N_EXPERTS, 1, 2 * D_EXPERT).astype(F32)
    b2r = b2.reshape(N_EXPERTS, 1, D_MODEL).astype(F32)
    grid_spec = pltpu.PrefetchScalarGridSpec(
        num_scalar_prefetch=2,
        grid=(n_blocks,),
        in_specs=[
            pl.BlockSpec((mb * ROW_TILES, LANES), lambda i, be, used: (jnp.minimum(i, used[0] - 1), 0)),
            pl.BlockSpec((1, D_MODEL, 2 * D_EXPERT), lambda i, be, used: (be[i], 0, 0)),
            pl.BlockSpec((1, 1, 2 * D_EXPERT), lambda i, be, used: (be[i], 0, 0)),
            pl.BlockSpec((1, D_EXPERT, D_MODEL), lambda i, be, used: (be[i], 0, 0)),
            pl.BlockSpec((1, 1, D_MODEL), lambda i, be, used: (be[i], 0, 0)),
        ],
        out_specs=pl.BlockSpec((mb * ROW_TILES, LANES), lambda i, be, used: (i, 0)),
        scratch_shapes=[pltpu.VMEM((D_MODEL, 2 * D_EXPERT), BF16), pltpu.VMEM((D_EXPERT, D_MODEL), BF16)],
    )
    return pl.pallas_call(
        functools.partial(_expert_kernel, mb=mb),
        grid_spec=grid_spec,
        out_shape=jax.ShapeDtypeStruct(xs.shape, F32),
        compiler_params=_params(("arbitrary",), vmem=EXPERT_VMEM_LIMIT),
        name="experts",
    )(block_expert, n_used, xs, w1, b1r, w2, b2r)


COMBINE_UNROLL = 4
COMBINE_CHUNK = 64


def _combine_kernel(dest_ref, dest_next_ref, x1_ref, gate_ref, fw_ref, y_ref, o_ref, buf_ref, sems, *, tc, tiles):
    i = pl.program_id(0)
    slot = i % 2

    def row_copy(d_ref, dst_slot, t, k):
        return pltpu.make_async_copy(_row_slice(y_ref, d_ref[0, 0, k * tc + t]),
                                     _row_slice(buf_ref.at[dst_slot, k], t), sems.at[dst_slot])

    def gather(d_ref, dst_slot):
        def body(g, carry):
            for u in range(COMBINE_UNROLL):
                for k in range(TOP_K):
                    row_copy(d_ref, dst_slot, g * COMBINE_UNROLL + u, k).start(priority=k % 2)
            return carry
        lax.fori_loop(0, tc // COMBINE_UNROLL, body, 0)

    @pl.when(i == 0)
    def _():
        gather(dest_ref, 0)

    def drain(dst_slot):
        def body(g, carry):
            for u in range(COMBINE_UNROLL):
                for k in range(TOP_K):
                    row_copy(dest_ref, dst_slot, g * COMBINE_UNROLL + u, k).wait()
            return carry
        lax.fori_loop(0, tc // COMBINE_UNROLL, body, 0)

    drain(slot)

    gates = gate_ref[...]
    gates_t = jnp.concatenate([gates, jnp.zeros((SUBLANES - TOP_K, tc), F32)], axis=0).T
    pieces = (tc // COMBINE_CHUNK) * ROW_TILES
    per_piece = tc // pieces
    piece = 0
    for c in range(tc // COMBINE_CHUNK):
        tok = slice(c * COMBINE_CHUNK, (c + 1) * COMBINE_CHUNK)
        gate_cols = [jnp.broadcast_to(gates_t[tok, k:k + 1], (COMBINE_CHUNK, LANES)) for k in range(TOP_K)]
        sum_sq = jnp.zeros((COMBINE_CHUNK, 1), F32)
        for s in range(ROW_TILES):
            slab = slice(s * LANES, (s + 1) * LANES)
            xo = x1_ref[tok, slab]
            for k in range(TOP_K):
                rows = pl.ds(c * COMBINE_CHUNK * ROW_TILES + s, COMBINE_CHUNK, stride=ROW_TILES)
                xo = xo + gate_cols[k] * buf_ref[slot, k, rows, :]
            o_ref[tok, slab] = xo
            sum_sq = sum_sq + jnp.sum(xo * xo, axis=-1, keepdims=True)
            for t in range(piece * per_piece, (piece + 1) * per_piece):
                for k in range(TOP_K):
                    row_copy(dest_next_ref, 1 - slot, t, k).start(priority=k % 2)
            piece += 1
        inv = lax.rsqrt(sum_sq * (1.0 / D_MODEL) + RMS_EPS)
        o_ref[tok, :] = o_ref[tok, :] * inv * fw_ref[...]

    @pl.when(i == tiles - 1)
    def _():
        drain(1 - slot)


def _combine(x1, gates, dest, y, final_w, tc):
    n = x1.shape[0]
    tiles = n // tc
    dest_t = dest.reshape(TOP_K, tiles, tc).transpose(1, 0, 2).reshape(tiles, 1, TOP_K * tc)
    fw = final_w.reshape(1, D_MODEL)
    dest_spec = lambda step: pl.BlockSpec((1, 1, TOP_K * tc), lambda i: (jnp.minimum(i + step, tiles - 1), 0, 0),
                                          memory_space=pltpu.SMEM)
    return pl.pallas_call(
        functools.partial(_combine_kernel, tc=tc, tiles=tiles),
        grid=(tiles,),
        in_specs=[dest_spec(0), dest_spec(1),
                  pl.BlockSpec((tc, D_MODEL), lambda i: (i, 0)),
                  pl.BlockSpec((TOP_K, tc), lambda i: (0, i)),
                  pl.BlockSpec((1, D_MODEL), lambda i: (0, 0)),
                  pl.BlockSpec(memory_space=pl.ANY)],
        out_specs=pl.BlockSpec((tc, D_MODEL), lambda i: (i, 0)),
        out_shape=jax.ShapeDtypeStruct((n, D_MODEL), F32),
        scratch_shapes=[pltpu.VMEM((2, TOP_K, tc * ROW_TILES, LANES), F32), pltpu.SemaphoreType.DMA((2,))],
        compiler_params=_params(("arbitrary",)),
        name="combine",
    )(dest_t, dest_t, x1, gates, fw, y)


PROJ_ROWS = 512
EXPERT_ROWS = 512
DISPATCH_ROWS = 2048
COMBINE_ROWS = 256


def _slot_layout(counts, n_pairs, mb):
    padded = (counts + mb - 1) // mb * mb
    pad_end = jnp.cumsum(padded)
    pad_start = pad_end - padded
    n_blocks = n_pairs // mb + N_EXPERTS
    block_row0 = jnp.arange(n_blocks, dtype=jnp.int32) * mb
    block_expert = jnp.minimum(
        jnp.sum((pad_end[None, :] <= block_row0[:, None]).astype(jnp.int32), axis=1), N_EXPERTS - 1)
    n_used = (pad_end[-1:] // mb).astype(jnp.int32)
    return pad_start.astype(jnp.int32), (pad_start + counts).astype(jnp.int32), (padded - counts).astype(jnp.int32), \
        block_expert, n_used, n_blocks


def _layer(x2, batch, seq, norm1_w, w_in, ret_norm_w, w_out, norm2_w, w_router, b_router, w1, b1, w2, b2,
           final_w):
    n = x2.shape[0]
    tm = min(PROJ_ROWS, seq)
    rq, rk, rv, rg, mq, mk, mvt = _in_projection(x2, norm1_w, w_in, batch, seq, tm)
    ret = _retention(rq, rk, rv, rg, ret_norm_w, batch, seq)
    moba = _moba(mq, mk, mvt, batch, seq)
    x1, h2_tiled, top_e, gates, rank, cnt = _route(x2, ret, moba, w_out, norm2_w, w_router, b_router, tm)
    counts = cnt[:, 0].astype(jnp.int32)
    mb = EXPERT_ROWS
    pad_start, fill_lo, fill_n, block_expert, n_used, n_blocks = _slot_layout(counts, n * TOP_K, mb)
    expert_ids = jnp.arange(N_EXPERTS, dtype=jnp.int32)[:, None, None]
    dest = rank + jnp.sum(jnp.where(top_e[None] == expert_ids, pad_start[:, None, None], 0), axis=0)
    xs = _dispatch(h2_tiled, dest, fill_lo, fill_n, n_blocks * mb, min(DISPATCH_ROWS, n))
    y = _experts(xs, block_expert, n_used, w1, b1, w2, b2, mb)
    return _combine(x1, gates, dest, y, final_w, min(COMBINE_ROWS, n))


def kernel(x, norm1_w, w_in, ret_norm_w, w_out, norm2_w, w_router, b_router, w1, b1, w2, b2, final_norm_w):
    batch, seq, d = x.shape
    depth = w_in.shape[0]
    assert depth == 1 and d == D_MODEL and seq % MOBA_BLOCK == 0
    x2 = x.reshape(batch * seq, d)
    out = _layer(x2, batch, seq, norm1_w[0], w_in[0], ret_norm_w[0], w_out[0], norm2_w[0], w_router[0],
                 b_router[0], w1[0], b1[0], w2[0], b2[0], final_norm_w)
    return out.reshape(batch, seq, d)
```

```python
import functools

import numpy as np
import jax
import jax.numpy as jnp
from jax import lax
from jax.experimental import pallas as pl
from jax.experimental.pallas import tpu as pltpu

D_MODEL = 1024
RET_HEADS = 4
RET_DK = 64
RET_DV = 128
RET_CHUNK = 128
RET_THETA = 10000.0
MOBA_HEADS = 8
MOBA_DH = 64
MOBA_BLOCK = 256
MOBA_TOPK = 3
ROPE_THETA = 500000.0
ROPE_DIM = MOBA_DH // 4
RET_QK_W = RET_HEADS * RET_DK
RET_V_W = RET_HEADS * RET_DV
MOBA_W = MOBA_HEADS * MOBA_DH
N_EXPERTS = 32
TOP_K = 4
D_EXPERT = D_MODEL
SWIGLU_LIMIT = 7.0
SWIGLU_ALPHA = 1.702
RMS_EPS = 1e-5
GN_EPS = 1e-6
NEG_INF = -1e30

SUBLANES = 8
LANES = 128
ROW_TILES = D_MODEL // LANES
VMEM_LIMIT = 48 * 1024 * 1024
EXPERT_VMEM_LIMIT = 56 * 1024 * 1024

F32 = jnp.float32
BF16 = jnp.bfloat16
NT_DIMS = (((1,), (1,)), ((), ()))
TN_DIMS = (((0,), (0,)), ((), ()))


def _params(sem, vmem=VMEM_LIMIT):
    return pltpu.CompilerParams(dimension_semantics=sem, vmem_limit_bytes=vmem)


def _rotate(x, cos, sin_lo, sin_hi, half):
    width = x.shape[-1]
    nxt = pltpu.roll(x, width - half, axis=1)
    prv = pltpu.roll(x, half, axis=1)
    return x * cos + nxt * sin_lo + prv * sin_hi


def _inproj_kernel(x_ref, nw_ref, w_ref, wvt_ref, rc_ref, rlo_ref, rhi_ref, mc_ref, mlo_ref, mhi_ref,
                   rq_ref, rk_ref, rv_ref, rg_ref, mq_ref, mk_ref, mvt_ref):
    x = x_ref[...]
    ms = jnp.mean(x * x, axis=-1, keepdims=True)
    h = (x * lax.rsqrt(ms + RMS_EPS) * nw_ref[...]).astype(BF16)
    proj = jnp.dot(h, w_ref[...], preferred_element_type=F32)
    o = 0
    rq = proj[:, o:o + RET_QK_W]; o += RET_QK_W
    rk = proj[:, o:o + RET_QK_W]; o += RET_QK_W
    rv = proj[:, o:o + RET_V_W]; o += RET_V_W
    rg = proj[:, o:o + RET_V_W]; o += RET_V_W
    mq = proj[:, o:o + MOBA_W]; o += MOBA_W
    mk = proj[:, o:o + MOBA_W]
    rc, rlo, rhi = rc_ref[...], rlo_ref[...], rhi_ref[...]
    mc, mlo, mhi = mc_ref[...], mlo_ref[...], mhi_ref[...]
    rq_ref[...] = _rotate(rq, rc, rlo, rhi, RET_DK // 2).astype(BF16)
    rk_ref[...] = _rotate(rk, rc, rlo, rhi, RET_DK // 2).astype(BF16)
    rv_ref[...] = rv.astype(BF16)
    rg_ref[...] = rg.astype(BF16)
    mq_ref[...] = (_rotate(mq, mc, mlo, mhi, ROPE_DIM // 2) * MOBA_Q_SCALE).astype(BF16)
    mk_ref[...] = _rotate(mk, mc, mlo, mhi, ROPE_DIM // 2).astype(BF16)
    mvt_ref[0] = lax.dot_general(wvt_ref[...], h, NT_DIMS, preferred_element_type=F32).astype(BF16)


def _rotary_tables(seq, rot_dim, theta, head_dim, heads):
    half = rot_dim // 2
    inv_freq = (np.float32(theta) ** (np.float32(-2.0) * np.arange(half, dtype=np.float32) / np.float32(rot_dim)))
    ang = np.arange(seq, dtype=np.float32)[:, None] * inv_freq.astype(np.float32)[None, :]
    cos, sin = np.cos(ang).astype(np.float32), np.sin(ang).astype(np.float32)
    rest = head_dim - rot_dim
    ones = np.ones((seq, rest), np.float32)
    zeros = np.zeros((seq, rest), np.float32)
    zh = np.zeros((seq, half), np.float32)
    cos_t = np.concatenate([cos, cos, ones], axis=-1)
    lo_t = np.concatenate([-sin, zh, zeros], axis=-1)
    hi_t = np.concatenate([zh, sin, zeros], axis=-1)
    tile = lambda t: jnp.asarray(np.tile(t, (1, heads)))
    return tile(cos_t), tile(lo_t), tile(hi_t)


def _in_projection(x2, norm_w, w_in, batch, seq, tm):
    n = x2.shape[0]
    st = seq // tm
    main_w = 2 * RET_QK_W + 2 * RET_V_W + 2 * MOBA_W
    w_main = w_in[:, :main_w].astype(BF16)
    w_vt = w_in[:, main_w:].T.astype(BF16)
    r_tabs = _rotary_tables(seq, RET_DK, RET_THETA, RET_DK, RET_HEADS)
    m_tabs = _rotary_tables(seq, ROPE_DIM, ROPE_THETA, MOBA_DH, MOBA_HEADS)
    row = lambda w: pl.BlockSpec((tm, w), lambda b, s: (b * st + s, 0))
    pos = lambda w: pl.BlockSpec((tm, w), lambda b, s: (s, 0))
    full = lambda a: pl.BlockSpec(a.shape, lambda b, s: (0,) * a.ndim)
    nw = norm_w.reshape(1, D_MODEL)
    out_shape = (
        jax.ShapeDtypeStruct((n, RET_QK_W), BF16), jax.ShapeDtypeStruct((n, RET_QK_W), BF16),
        jax.ShapeDtypeStruct((n, RET_V_W), BF16), jax.ShapeDtypeStruct((n, RET_V_W), BF16),
        jax.ShapeDtypeStruct((n, MOBA_W), BF16), jax.ShapeDtypeStruct((n, MOBA_W), BF16),
        jax.ShapeDtypeStruct((batch, MOBA_W, seq), BF16),
    )
    return pl.pallas_call(
        _inproj_kernel,
        grid=(batch, st),
        in_specs=[row(D_MODEL), full(nw), full(w_main), full(w_vt),
                  pos(RET_QK_W), pos(RET_QK_W), pos(RET_QK_W), pos(MOBA_W), pos(MOBA_W), pos(MOBA_W)],
        out_specs=(row(RET_QK_W), row(RET_QK_W), row(RET_V_W), row(RET_V_W), row(MOBA_W), row(MOBA_W),
                   pl.BlockSpec((1, MOBA_W, tm), lambda b, s: (b, 0, s))),
        out_shape=out_shape,
        compiler_params=_params(("parallel", "parallel")),
        name="in_projection",
    )(x2, nw, w_main, w_vt, *r_tabs, *m_tabs)


RET_UNROLL = 4


def _retention_kernel(q_ref, k_ref, v_ref, g_ref, decay_ref, xi_ref, zeta_ref, gnw_ref, o_ref, state_ref,
                      *, n_chunks, chunk_decay):
    c = RET_CHUNK
    state_ref[...] = jnp.zeros_like(state_ref)

    def chunk_group(gi, carry):
        states = [state_ref[h] for h in range(RET_HEADS)]
        for u in range(RET_UNROLL):
            r0 = pl.multiple_of((gi * RET_UNROLL + u) * c, c)
            q = q_ref[pl.ds(r0, c), :]
            k = k_ref[pl.ds(r0, c), :]
            v = v_ref[pl.ds(r0, c), :]
            g = g_ref[pl.ds(r0, c), :].astype(F32)
            qx = (q.astype(F32) * xi_ref[...]).astype(BF16)
            kz = (k.astype(F32) * zeta_ref[...]).astype(BF16)
            outs = []
            for h in range(RET_HEADS):
                qs = slice(h * RET_DK, (h + 1) * RET_DK)
                vs = slice(h * RET_DV, (h + 1) * RET_DV)
                vh = v[:, vs]
                scores = lax.dot_general(q[:, qs], k[:, qs], NT_DIMS, preferred_element_type=F32) * decay_ref[h]
                inner = jnp.dot(scores.astype(BF16), vh, preferred_element_type=F32)
                cross = jnp.dot(qx[:, qs], states[h].astype(BF16), preferred_element_type=F32)
                states[h] = chunk_decay[h] * states[h] + lax.dot_general(
                    kz[:, qs], vh, TN_DIMS, preferred_element_type=F32)
                ret = inner + cross
                mu = jnp.mean(ret, axis=-1, keepdims=True)
                cen = ret - mu
                var = jnp.mean(cen * cen, axis=-1, keepdims=True)
                outs.append(cen * lax.rsqrt(var + GN_EPS))
            normed = jnp.concatenate(outs, axis=-1) * gnw_ref[...]
            o_ref[pl.ds(r0, c), :] = (g * jax.nn.sigmoid(g) * normed).astype(BF16)
        for h in range(RET_HEADS):
            state_ref[h] = states[h]
        return carry

    lax.fori_loop(0, n_chunks // RET_UNROLL, chunk_group, 0)


def _retention_consts():
    h = np.arange(RET_HEADS, dtype=np.float32)
    log_g = np.log1p(-np.exp2(-5.0 - h)).astype(np.float32)
    idx = np.arange(RET_CHUNK, dtype=np.float32)
    rel = idx[:, None] - idx[None, :]
    scale = np.float32(RET_DK ** -0.5)
    decay = np.where(rel >= 0, np.exp(log_g[:, None, None] * np.maximum(rel, 0.0)), 0.0).astype(np.float32) * scale
    zeta = np.exp(log_g[:, None] * (RET_CHUNK - 1.0 - idx)).astype(np.float32) * scale
    xi = np.exp(log_g[:, None] * (idx + 1.0)).astype(np.float32)
    spread = lambda t: np.repeat(t.T, RET_DK, axis=1)
    chunk_decay = tuple(float(v) for v in np.exp(log_g * RET_CHUNK).astype(np.float32))
    return decay, spread(xi), spread(zeta), chunk_decay


def _retention(rq, rk, rv, rg, gn_w, batch, seq):
    n = rq.shape[0]
    decay, xi, zeta, chunk_decay = _retention_consts()
    gnw = gn_w.reshape(1, RET_V_W).astype(F32)
    seq_blk = lambda w: pl.BlockSpec((seq, w), lambda b: (b, 0))
    full = lambda a: pl.BlockSpec(a.shape, lambda b: (0,) * a.ndim)
    kern = functools.partial(_retention_kernel, n_chunks=seq // RET_CHUNK, chunk_decay=chunk_decay)
    return pl.pallas_call(
        kern,
        grid=(batch,),
        in_specs=[seq_blk(RET_QK_W), seq_blk(RET_QK_W), seq_blk(RET_V_W), seq_blk(RET_V_W),
                  full(decay), full(xi), full(zeta), full(gnw)],
        out_specs=seq_blk(RET_V_W),
        out_shape=jax.ShapeDtypeStruct((n, RET_V_W), BF16),
        scratch_shapes=[pltpu.VMEM((RET_HEADS, RET_DK, RET_DV), F32)],
        compiler_params=_params(("parallel",)),
        name="retention",
    )(rq, rk, rv, rg, jnp.asarray(decay), jnp.asarray(xi), jnp.asarray(zeta), gnw)


HEADS_PER_STEP = LANES // MOBA_DH
MOBA_Q_SCALE = MOBA_DH ** -0.5 * float(np.log2(np.e))
ONES_ROWS = 16


def _moba_kernel(q_ref, k_ref, vt_ref, o_ref, *, n_blocks):
    blk = MOBA_BLOCK
    dh = MOBA_DH
    blk_id = lax.broadcasted_iota(jnp.int32, (n_blocks, blk), 0)
    key_pos = lax.broadcasted_iota(jnp.int32, (blk, blk), 0)
    qry_pos = lax.broadcasted_iota(jnp.int32, (blk, blk), 1)
    causal = key_pos <= qry_pos
    rows = lambda b: slice(b * blk, (b + 1) * blk)

    def masked_scores(hh, j, k_mean):
        lanes = slice(hh * dh, (hh + 1) * dh)
        q = q_ref[rows(j), lanes]
        n_keys = (j + 1) * blk
        s_all = lax.dot_general(k_ref[0:n_keys, lanes], q, NT_DIMS, preferred_element_type=F32)
        if j > 0:
            gate = lax.dot_general(k_mean, q.astype(F32), NT_DIMS, preferred_element_type=F32)
            past = blk_id < j
            gate = jnp.where(past, gate, NEG_INF)
            rank = jnp.zeros((n_blocks, blk), jnp.int32)
            for other in range(j):
                g_o = gate[other:other + 1, :]
                ahead = (g_o > gate) | ((g_o == gate) & (other < blk_id))
                rank = rank + ahead.astype(jnp.int32)
            bias = jnp.where(past & (rank < MOBA_TOPK), 0.0, NEG_INF)
        s_blocks = [s_all[rows(i), :] + bias[i:i + 1, :] for i in range(j)]
        s_blocks.append(jnp.where(causal, s_all[rows(j), :], NEG_INF))
        m = jnp.max(s_blocks[0], axis=0, keepdims=True)
        for s in s_blocks[1:]:
            m = jnp.maximum(m, jnp.max(s, axis=0, keepdims=True))
        return s_blocks, m

    def softmax_weights(s_blocks, m):
        return jnp.concatenate([jnp.exp2(s - m).astype(BF16) for s in s_blocks], axis=0)

    def weighted_values(hh, j, p_all):
        lanes = slice(hh * dh, (hh + 1) * dh)
        n_keys = (j + 1) * blk
        values = jnp.concatenate([vt_ref[0, lanes, 0:n_keys], jnp.ones((ONES_ROWS, n_keys), BF16)], axis=0)
        acc = jnp.dot(values, p_all, preferred_element_type=F32)
        return (acc[:dh] / acc[dh:dh + 1]).T

    k_means = []
    for hh in range(HEADS_PER_STEP):
        kf = k_ref[:, hh * dh:(hh + 1) * dh].astype(F32)
        k_means.append(jnp.mean(kf.reshape(n_blocks, blk, dh), axis=1))
    units = [(j, hh) for j in range(n_blocks) for hh in range(HEADS_PER_STEP)]
    scored, weighted, outs = {}, {}, {}
    for step in range(len(units) + 2):
        if step < len(units):
            j, hh = units[step]
            scored[step] = masked_scores(hh, j, k_means[hh])
        if 0 <= step - 1 < len(units):
            weighted[step - 1] = softmax_weights(*scored.pop(step - 1))
        if 0 <= step - 2 < len(units):
            j, hh = units[step - 2]
            outs[(j, hh)] = weighted_values(hh, j, weighted.pop(step - 2))
            if hh == HEADS_PER_STEP - 1:
                o_ref[rows(j), :] = jnp.concatenate(
                    [outs.pop((j, h)) for h in range(HEADS_PER_STEP)], axis=-1).astype(BF16)


def _moba(mq, mk, mvt, batch, seq):
    n = mq.shape[0]
    n_blocks = seq // MOBA_BLOCK
    pairs = MOBA_HEADS // HEADS_PER_STEP
    qk_spec = pl.BlockSpec((seq, LANES), lambda b, p: (b, p))
    kern = functools.partial(_moba_kernel, n_blocks=n_blocks)
    return pl.pallas_call(
        kern,
        grid=(batch, pairs),
        in_specs=[qk_spec, qk_spec, pl.BlockSpec((1, LANES, seq), lambda b, p: (b, p, 0))],
        out_specs=qk_spec,
        out_shape=jax.ShapeDtypeStruct((n, MOBA_W), BF16),
        compiler_params=_params(("parallel", "parallel")),
        name="moba",
    )(mq, mk, mvt)


def _store_row_tiled(ref, val):
    rows = val.shape[0]
    for s in range(ROW_TILES):
        ref[pl.ds(s, rows, stride=ROW_TILES), :] = val[:, s * LANES:(s + 1) * LANES]


def _load_row_tiled(ref, rows, lead=()):
    parts = []
    for s in range(ROW_TILES):
        parts.append(ref[tuple(lead) + (pl.ds(s, rows, stride=ROW_TILES), slice(None))])
    return jnp.concatenate(parts, axis=-1)


def _route_kernel(x_ref, ret_ref, moba_ref, wor_ref, wom_ref, nw_ref, wrt_ref, br_ref, tri_ref,
                  x1_ref, h2_ref, te_ref, gate_ref, rank_ref, cnt_ref, carry_ref):
    @pl.when(pl.program_id(0) == 0)
    def _():
        carry_ref[...] = jnp.zeros_like(carry_ref)

    x1 = (x_ref[...]
          + jnp.dot(ret_ref[...], wor_ref[...], preferred_element_type=F32)
          + jnp.dot(moba_ref[...], wom_ref[...], preferred_element_type=F32))
    x1_ref[...] = x1
    ms = jnp.mean(x1 * x1, axis=-1, keepdims=True)
    h2 = x1 * lax.rsqrt(ms + RMS_EPS) * nw_ref[...]
    _store_row_tiled(h2_ref, h2)
    h_hi = h2.astype(BF16)
    h_lo = (h2 - h_hi.astype(F32)).astype(BF16)
    part = lambda w, h: lax.dot_general(w, h, NT_DIMS, preferred_element_type=F32)
    logits = part(wrt_ref[0], h_hi) + (part(wrt_ref[0], h_lo) + part(wrt_ref[1], h_hi)) + br_ref[...]
    tm = logits.shape[1]
    e_id = lax.broadcasted_iota(jnp.int32, (N_EXPERTS, tm), 0)
    vals, ids = [], []
    for _ in range(TOP_K):
        m = jnp.max(logits, axis=0, keepdims=True)
        idx = jnp.min(jnp.where(logits == m, e_id, N_EXPERTS), axis=0, keepdims=True)
        vals.append(m)
        ids.append(idx)
        logits = jnp.where(e_id == idx, -jnp.inf, logits)
    exps = [jnp.exp(v - vals[0]) for v in vals]
    denom = exps[0] + exps[1] + exps[2] + exps[3]
    gate_ref[...] = jnp.concatenate([e / denom for e in exps], axis=0)
    te_ref[...] = jnp.concatenate(ids, axis=0)
    hits = [e_id == idx for idx in ids]
    picked = (hits[0] | hits[1] | hits[2] | hits[3])
    picked_f = jnp.where(picked, 1.0, 0.0)
    before = jnp.dot(picked_f.astype(BF16), tri_ref[...], preferred_element_type=F32) + carry_ref[...]
    ranks = [jnp.sum(jnp.where(hit, before, 0.0), axis=0, keepdims=True) for hit in hits]
    rank_ref[...] = jnp.concatenate(ranks, axis=0).astype(jnp.int32)
    carry_ref[...] = carry_ref[...] + jnp.sum(picked_f, axis=1, keepdims=True)
    cnt_ref[...] = jnp.broadcast_to(carry_ref[...], cnt_ref.shape)


def _route(x2, ret, moba, w_out, norm_w, w_router, b_router, tm):
    n = x2.shape[0]
    wo_r = w_out[:RET_V_W].astype(BF16)
    wo_m = w_out[RET_V_W:].astype(BF16)
    nw = norm_w.reshape(1, D_MODEL)
    wr_f32 = w_router.T.astype(F32)
    wr_hi = wr_f32.astype(BF16)
    wrt = jnp.stack([wr_hi, (wr_f32 - wr_hi.astype(F32)).astype(BF16)])
    br = b_router.reshape(N_EXPERTS, 1).astype(F32)
    tri = (np.arange(tm)[:, None] < np.arange(tm)[None, :]).astype(np.float32)
    tri = jnp.asarray(tri, dtype=BF16)
    row = lambda w: pl.BlockSpec((tm, w), lambda i: (i, 0))
    col = pl.BlockSpec((TOP_K, tm), lambda i: (0, i))
    full = lambda a: pl.BlockSpec(a.shape, lambda i: (0,) * a.ndim)
    out_shape = (
        jax.ShapeDtypeStruct((n, D_MODEL), F32),
        jax.ShapeDtypeStruct((n * ROW_TILES, LANES), F32),
        jax.ShapeDtypeStruct((TOP_K, n), jnp.int32),
        jax.ShapeDtypeStruct((TOP_K, n), F32),
        jax.ShapeDtypeStruct((TOP_K, n), jnp.int32),
        jax.ShapeDtypeStruct((N_EXPERTS, LANES), F32),
    )
    return pl.pallas_call(
        _route_kernel,
        grid=(n // tm,),
        in_specs=[row(D_MODEL), row(RET_V_W), row(MOBA_W), full(wo_r), full(wo_m), full(nw), full(wrt),
                  full(br), full(tri)],
        out_specs=(row(D_MODEL), pl.BlockSpec((tm * ROW_TILES, LANES), lambda i: (i, 0)), col, col, col,
                   pl.BlockSpec((N_EXPERTS, LANES), lambda i: (0, 0))),
        out_shape=out_shape,
        scratch_shapes=[pltpu.VMEM((N_EXPERTS, 1), F32)],
        compiler_params=_params(("arbitrary",)),
        name="out_proj_route",
    )(x2, ret, moba, wo_r, wo_m, nw, wrt, br, tri)


def _row_slice(ref, row):
    return ref.at[pl.ds(pl.multiple_of(row * ROW_TILES, ROW_TILES), ROW_TILES)]


DISPATCH_UNROLL = 4


def _dispatch_kernel(fill_lo_ref, fill_n_ref, dest_ref, h2_ref, xs_ref, zero_ref, sem, fill_sem, *, tt):
    @pl.when(pl.program_id(0) == 0)
    def _():
        zero_ref[...] = jnp.zeros_like(zero_ref)

        def fill_copy(e, r):
            return pltpu.make_async_copy(zero_ref, _row_slice(xs_ref, fill_lo_ref[e] + r), fill_sem)

        def fill(e, carry):
            def one(r, c):
                fill_copy(e, r).start()
                return c
            return lax.fori_loop(0, fill_n_ref[e], one, carry)

        def fill_wait(e, carry):
            def one(r, c):
                fill_copy(e, r).wait()
                return c
            return lax.fori_loop(0, fill_n_ref[e], one, carry)

        lax.fori_loop(0, N_EXPERTS, fill, 0)
        lax.fori_loop(0, N_EXPERTS, fill_wait, 0)

    def row_copy(t, k):
        return pltpu.make_async_copy(_row_slice(h2_ref, t), _row_slice(xs_ref, dest_ref[0, 0, k * tt + t]), sem)

    def issue(g, carry):
        for u in range(DISPATCH_UNROLL):
            for k in range(TOP_K):
                row_copy(g * DISPATCH_UNROLL + u, k).start(priority=k % 2)
        return carry

    lax.fori_loop(0, tt // DISPATCH_UNROLL, issue, 0)

    def drain(g, carry):
        for u in range(DISPATCH_UNROLL):
            for k in range(TOP_K):
                row_copy(g * DISPATCH_UNROLL + u, k).wait()
        return carry

    lax.fori_loop(0, tt // DISPATCH_UNROLL, drain, 0)


def _dispatch(h2_tiled, dest, fill_lo, fill_n, n_slots, tt):
    n = dest.shape[1]
    tiles = n // tt
    dest_t = dest.reshape(TOP_K, tiles, tt).transpose(1, 0, 2).reshape(tiles, 1, TOP_K * tt)
    grid_spec = pltpu.PrefetchScalarGridSpec(
        num_scalar_prefetch=2,
        grid=(tiles,),
        in_specs=[pl.BlockSpec((1, 1, TOP_K * tt), lambda i, lo, cnt: (i, 0, 0), memory_space=pltpu.SMEM),
                  pl.BlockSpec((tt * ROW_TILES, LANES), lambda i, lo, cnt: (i, 0))],
        out_specs=pl.BlockSpec(memory_space=pl.ANY),
        scratch_shapes=[pltpu.VMEM((ROW_TILES, LANES), F32), pltpu.SemaphoreType.DMA(()),
                        pltpu.SemaphoreType.DMA(())],
    )
    return pl.pallas_call(
        functools.partial(_dispatch_kernel, tt=tt),
        grid_spec=grid_spec,
        out_shape=jax.ShapeDtypeStruct((n_slots * ROW_TILES, LANES), F32),
        compiler_params=_params(("arbitrary",)),
        name="dispatch",
    )(fill_lo, fill_n, dest_t, h2_tiled)


def _expert_kernel(be_ref, used_ref, xs_ref, w1_ref, b1_ref, w2_ref, b2_ref, y_ref, w1b_ref, w2b_ref, *, mb):
    b = pl.program_id(0)

    @pl.when((b == 0) | (be_ref[b] != be_ref[jnp.maximum(b - 1, 0)]))
    def _():
        w1b_ref[...] = w1_ref[0].astype(BF16)
        w2b_ref[...] = w2_ref[0].astype(BF16)

    @pl.when(b < used_ref[0])
    def _():
        x = _load_row_tiled(xs_ref, mb).astype(BF16)
        u = jnp.dot(x, w1b_ref[...], preferred_element_type=F32) + b1_ref[0]
        gate = jnp.minimum(u[:, :D_EXPERT], SWIGLU_LIMIT)
        lin = jnp.clip(u[:, D_EXPERT:], -SWIGLU_LIMIT, SWIGLU_LIMIT)
        act = gate * jax.nn.sigmoid(SWIGLU_ALPHA * gate) * (lin + 1.0)
        y = jnp.dot(act.astype(BF16), w2b_ref[...], preferred_element_type=F32) + b2_ref[0]
        _store_row_tiled(y_ref, y)

    @pl.when(b >= used_ref[0])
    def _():
        y_ref[...] = jnp.zeros_like(y_ref)


def _experts(xs, block_expert, n_used, w1, b1, w2, b2, mb):
    n_blocks = block_expert.shape[0]
    b1r = b1.reshape(N_EXPERTS, 1, 2 * D_EXPERT).astype(F32)
    b2r = b2.reshape(N_EXPERTS, 1, D_MODEL).astype(F32)
    grid_spec = pltpu.PrefetchScalarGridSpec(
        num_scalar_prefetch=2,
        grid=(n_blocks,),
        in_specs=[
            pl.BlockSpec((mb * ROW_TILES, LANES), lambda i, be, used: (jnp.minimum(i, used[0] - 1), 0)),
            pl.BlockSpec((1, D_MODEL, 2 * D_EXPERT), lambda i, be, used: (be[i], 0, 0)),
            pl.BlockSpec((1, 1, 2 * D_EXPERT), lambda i, be, used: (be[i], 0, 0)),
            pl.BlockSpec((1, D_EXPERT, D_MODEL), lambda i, be, used: (be[i], 0, 0)),
            pl.BlockSpec((1, 1, D_MODEL), lambda i, be, used: (be[i], 0, 0)),
        ],
        out_specs=pl.BlockSpec((mb * ROW_TILES, LANES), lambda i, be, used: (i, 0)),
        scratch_shapes=[pltpu.VMEM((D_MODEL, 2 * D_EXPERT), BF16), pltpu.VMEM((D_EXPERT, D_MODEL), BF16)],
    )
    return pl.pallas_call(
        functools.partial(_expert_kernel, mb=mb),
        grid_spec=grid_spec,
        out_shape=jax.ShapeDtypeStruct(xs.shape, F32),
        compiler_params=_params(("arbitrary",), vmem=EXPERT_VMEM_LIMIT),
        name="experts",
    )(block_expert, n_used, xs, w1, b1r, w2, b2r)


COMBINE_UNROLL = 4
COMBINE_CHUNK = 64


def _combine_kernel(dest_ref, dest_next_ref, x1_ref, gate_ref, fw_ref, y_ref, o_ref, buf_ref, sems, *, tc, tiles):
    i = pl.program_id(0)
    slot = i % 2

    def row_copy(d_ref, dst_slot, t, k):
        return pltpu.make_async_copy(_row_slice(y_ref, d_ref[0, 0, k * tc + t]),
                                     _row_slice(buf_ref.at[dst_slot, k], t), sems.at[dst_slot])

    def gather(d_ref, dst_slot):
        def body(g, carry):
            for u in range(COMBINE_UNROLL):
                for k in range(TOP_K):
                    row_copy(d_ref, dst_slot, g * COMBINE_UNROLL + u, k).start(priority=k % 2)
            return carry
        lax.fori_loop(0, tc // COMBINE_UNROLL, body, 0)

    @pl.when(i == 0)
    def _():
        gather(dest_ref, 0)

    def drain(dst_slot):
        def body(g, carry):
            for u in range(COMBINE_UNROLL):
                for k in range(TOP_K):
                    row_copy(dest_ref, dst_slot, g * COMBINE_UNROLL + u, k).wait()
            return carry
        lax.fori_loop(0, tc // COMBINE_UNROLL, body, 0)

    drain(slot)

    gates = gate_ref[...]
    gates_t = jnp.concatenate([gates, jnp.zeros((SUBLANES - TOP_K, tc), F32)], axis=0).T
    pieces = (tc // COMBINE_CHUNK) * ROW_TILES
    per_piece = tc // pieces
    piece = 0
    for c in range(tc // COMBINE_CHUNK):
        tok = slice(c * COMBINE_CHUNK, (c + 1) * COMBINE_CHUNK)
        gate_cols = [jnp.broadcast_to(gates_t[tok, k:k + 1], (COMBINE_CHUNK, LANES)) for k in range(TOP_K)]
        sum_sq = jnp.zeros((COMBINE_CHUNK, 1), F32)
        for s in range(ROW_TILES):
            slab = slice(s * LANES, (s + 1) * LANES)
            xo = x1_ref[tok, slab]
            for k in range(TOP_K):
                rows = pl.ds(c * COMBINE_CHUNK * ROW_TILES + s, COMBINE_CHUNK, stride=ROW_TILES)
                xo = xo + gate_cols[k] * buf_ref[slot, k, rows, :]
            o_ref[tok, slab] = xo
            sum_sq = sum_sq + jnp.sum(xo * xo, axis=-1, keepdims=True)
            for t in range(piece * per_piece, (piece + 1) * per_piece):
                for k in range(TOP_K):
                    row_copy(dest_next_ref, 1 - slot, t, k).start(priority=k % 2)
            piece += 1
        inv = lax.rsqrt(sum_sq * (1.0 / D_MODEL) + RMS_EPS)
        o_ref[tok, :] = o_ref[tok, :] * inv * fw_ref[...]

    @pl.when(i == tiles - 1)
    def _():
        drain(1 - slot)


def _combine(x1, gates, dest, y, final_w, tc):
    n = x1.shape[0]
    tiles = n // tc
    dest_t = dest.reshape(TOP_K, tiles, tc).transpose(1, 0, 2).reshape(tiles, 1, TOP_K * tc)
    fw = final_w.reshape(1, D_MODEL)
    dest_spec = lambda step: pl.BlockSpec((1, 1, TOP_K * tc), lambda i: (jnp.minimum(i + step, tiles - 1), 0, 0),
                                          memory_space=pltpu.SMEM)
    return pl.pallas_call(
        functools.partial(_combine_kernel, tc=tc, tiles=tiles),
        grid=(tiles,),
        in_specs=[dest_spec(0), dest_spec(1),
                  pl.BlockSpec((tc, D_MODEL), lambda i: (i, 0)),
                  pl.BlockSpec((TOP_K, tc), lambda i: (0, i)),
                  pl.BlockSpec((1, D_MODEL), lambda i: (0, 0)),
                  pl.BlockSpec(memory_space=pl.ANY)],
        out_specs=pl.BlockSpec((tc, D_MODEL), lambda i: (i, 0)),
        out_shape=jax.ShapeDtypeStruct((n, D_MODEL), F32),
        scratch_shapes=[pltpu.VMEM((2, TOP_K, tc * ROW_TILES, LANES), F32), pltpu.SemaphoreType.DMA((2,))],
        compiler_params=_params(("arbitrary",)),
        name="combine",
    )(dest_t, dest_t, x1, gates, fw, y)


PROJ_ROWS = 512
EXPERT_ROWS = 512
DISPATCH_ROWS = 2048
COMBINE_ROWS = 256


def _slot_layout(counts, n_pairs, mb):
    padded = (counts + mb - 1) // mb * mb
    pad_end = jnp.cumsum(padded)
    pad_start = pad_end - padded
    n_blocks = n_pairs // mb + N_EXPERTS
    block_row0 = jnp.arange(n_blocks, dtype=jnp.int32) * mb
    block_expert = jnp.minimum(
        jnp.sum((pad_end[None, :] <= block_row0[:, None]).astype(jnp.int32), axis=1), N_EXPERTS - 1)
    n_used = (pad_end[-1:] // mb).astype(jnp.int32)
    return pad_start.astype(jnp.int32), (pad_start + counts).astype(jnp.int32), (padded - counts).astype(jnp.int32), \
        block_expert, n_used, n_blocks


def _layer(x2, batch, seq, norm1_w, w_in, ret_norm_w, w_out, norm2_w, w_router, b_router, w1, b1, w2, b2,
           final_w):
    n = x2.shape[0]
    tm = min(PROJ_ROWS, seq)
    rq, rk, rv, rg, mq, mk, mvt = _in_projection(x2, norm1_w, w_in, batch, seq, tm)
    ret = _retention(rq, rk, rv, rg, ret_norm_w, batch, seq)
    moba = _moba(mq, mk, mvt, batch, seq)
    x1, h2_tiled, top_e, gates, rank, cnt = _route(x2, ret, moba, w_out, norm2_w, w_router, b_router, tm)
    counts = cnt[:, 0].astype(jnp.int32)
    mb = EXPERT_ROWS
    pad_start, fill_lo, fill_n, block_expert, n_used, n_blocks = _slot_layout(counts, n * TOP_K, mb)
    expert_ids = jnp.arange(N_EXPERTS, dtype=jnp.int32)[:, None, None]
    dest = rank + jnp.sum(jnp.where(top_e[None] == expert_ids, pad_start[:, None, None], 0), axis=0)
    xs = _dispatch(h2_tiled, dest, fill_lo, fill_n, n_blocks * mb, min(DISPATCH_ROWS, n))
    y = _experts(xs, block_expert, n_used, w1, b1, w2, b2, mb)
    return _combine(x1, gates, dest, y, final_w, min(COMBINE_ROWS, n))


def kernel(x, norm1_w, w_in, ret_norm_w, w_out, norm2_w, w_router, b_router, w1, b1, w2, b2, final_norm_w):
    batch, seq, d = x.shape
    depth = w_in.shape[0]
    assert depth == 1 and d == D_MODEL and seq % MOBA_BLOCK == 0
    x2 = x.reshape(batch * seq, d)
    out = _layer(x2, batch, seq, norm1_w[0], w_in[0], ret_norm_w[0], w_out[0], norm2_w[0], w_router[0],
                 b_router[0], w1[0], b1[0], w2[0], b2[0], final_norm_w)
    return out.reshape(batch, seq, d)
```
